```python
import math
import jax, jax.numpy as jnp
from jax import lax
import numpy as np

D_MODEL = 4096
BATCH = 4
SEQ = 4096
DEPTH = 1

HEAD_DIM = 128
N_HEADS = D_MODEL // HEAD_DIM
HEADS_A = N_HEADS // 2
HEADS_B = N_HEADS - HEADS_A
WIDTH_A = HEADS_A * HEAD_DIM
WIDTH_B = HEADS_B * HEAD_DIM
MIX_WIDTH = WIDTH_A + WIDTH_B
DILATED_BRANCHES = ((128, 1), (512, 4), (2048, 16))
BLOCK_Q = 128
NUM_BUCKETS = 32
MAX_DISTANCE = 2048
FGATE_BIAS_MEAN = 2.0
PEER_HEADS = 8
N_KEYS = 128
N_EXPERTS = N_KEYS * N_KEYS
PEER_TOPK = 16
PEER_QUERY_DIM = 256
PEER_HALF = PEER_QUERY_DIM // 2
PEER_CHUNK = 64
N_MOD = 6
NORM_EPS = 1e-6
NEG_INF = -1e30
IN_COLS = 3 * WIDTH_A + 3 * WIDTH_B + HEADS_B

kernel_name = 'hybrid_dilated_fox_peer_layer'


def rmsnorm(x, g):
    x32 = x.astype(jnp.float32)
    y = x32 * lax.rsqrt(jnp.mean(x32 * x32, axis=-1, keepdims=True) + NORM_EPS)
    return (y * g.astype(jnp.float32)).astype(x.dtype)


def t5_bucket(dist):
    max_exact = NUM_BUCKETS // 2
    d32 = jnp.maximum(dist, 1).astype(jnp.float32)
    large = max_exact + (jnp.log(d32 / max_exact) / math.log(MAX_DISTANCE / max_exact)
                         * (NUM_BUCKETS - max_exact)).astype(jnp.int32)
    large = jnp.minimum(large, NUM_BUCKETS - 1)
    return jnp.where(dist < max_exact, dist, large)


def dilated_branch(q, k, v, rel_bias, window, dilation):
    B, H, S, Dh = q.shape
    L = S // dilation
    nw = window // dilation
    bq = min(BLOCK_Q, L)
    nblk = L // bq

    def to_sub(t):
        return t.reshape(B, H, L, dilation, Dh).transpose(0, 1, 3, 2, 4)

    qs = to_sub(q).reshape(B, H, dilation, nblk, bq, Dh)
    pad = ((0, 0), (0, 0), (0, 0), (nw, 0), (0, 0))
    ks = jnp.pad(to_sub(k), pad)
    vs = jnp.pad(to_sub(v), pad)
    idx = jnp.arange(nblk)[:, None] * bq + jnp.arange(bq + nw)[None, :]
    kb = ks[:, :, :, idx]
    vb = vs[:, :, :, idx].astype(jnp.float32)
    logits = jnp.einsum('bhrnid,bhrnjd->bhrnij', qs, kb).astype(jnp.float32) * (HEAD_DIM ** -0.5)
    rel = jnp.arange(bq)[:, None] + nw - jnp.arange(bq + nw)[None, :]
    in_win = (rel >= 0) & (rel <= nw)
    bucket = t5_bucket(jnp.clip(rel, 0, nw) * dilation)
    bias = jnp.transpose(rel_bias[bucket], (2, 0, 1)).astype(jnp.float32)
    key_ok = (idx - nw) >= 0
    mask = in_win[None, :, :] & key_ok[:, None, :]
    logits = jnp.where(mask, logits + bias[:, None, None], NEG_INF)
    m = jnp.max(logits, axis=-1, keepdims=True)
    p = jnp.exp(logits - m)
    denom = jnp.sum(p, axis=-1, keepdims=True)
    num = jnp.einsum('bhrnij,bhrnjd->bhrnid', p, vb)

    def from_sub(t):
        return t.reshape(B, H, dilation, L, -1).transpose(0, 1, 3, 2, 4).reshape(B, H, S, -1)

    return from_sub(num), from_sub(m), from_sub(denom)


def dilated_attention(q, k, v, rel_bias):
    outs = [dilated_branch(q, k, v, rel_bias, w, d) for (w, d) in DILATED_BRANCHES]
    m_all = outs[0][1]
    for o in outs[1:]:
        m_all = jnp.maximum(m_all, o[1])
    num = outs[0][0] * jnp.exp(outs[0][1] - m_all)
    den = outs[0][2] * jnp.exp(outs[0][1] - m_all)
    for o in outs[1:]:
        w = jnp.exp(o[1] - m_all)
        num = num + o[0] * w
        den = den + o[2] * w
    return num / den


def forgetting_attention(q, k, v, log_f):
    B, H, S, Dh = q.shape
    nq = S // BLOCK_Q
    F = jnp.cumsum(log_f, axis=-1)
    k32 = k.astype(jnp.float32)
    v32 = v.astype(jnp.float32)
    qb = q.astype(jnp.float32).reshape(B, H, nq, BLOCK_Q, Dh).transpose(2, 0, 1, 3, 4)
    Fq = F.reshape(B, H, nq, BLOCK_Q).transpose(2, 0, 1, 3)
    starts = jnp.arange(nq, dtype=jnp.int32) * BLOCK_Q
    kpos = jnp.arange(S, dtype=jnp.int32)
    scale = HEAD_DIM ** -0.5

    def block(args):
        qi, Fi, s0 = args
        logits = jnp.einsum('bhid,bhjd->bhij', qi, k32) * scale + Fi[..., :, None] - F[..., None, :]
        causal = kpos[None, :] <= (s0 + jnp.arange(BLOCK_Q, dtype=jnp.int32))[:, None]
        p = jax.nn.softmax(jnp.where(causal, logits, NEG_INF), axis=-1)
        return jnp.einsum('bhij,bhjd->bhid', p, v32)

    out = lax.map(block, (qb, Fq, starts))
    return out.transpose(1, 2, 0, 3, 4).reshape(B, H, S, Dh)


def peer(h, w_pq, sk1, sk2, expert_u, expert_v):
    B, S, D = h.shape
    T = B * S
    ht = h.reshape(T, D)
    q = jnp.dot(ht, w_pq).reshape(T, PEER_HEADS, 2, PEER_HALF)
    s1 = jnp.einsum('thd,kd->thk', q[:, :, 0], sk1).astype(jnp.float32)
    s2 = jnp.einsum('thd,kd->thk', q[:, :, 1], sk2).astype(jnp.float32)
    v1, i1 = lax.top_k(s1, PEER_TOPK)
    v2, i2 = lax.top_k(s2, PEER_TOPK)
    cand = (v1[..., :, None] + v2[..., None, :]).reshape(T, PEER_HEADS, PEER_TOPK * PEER_TOPK)
    vs, ci = lax.top_k(cand, PEER_TOPK)
    e1 = jnp.take_along_axis(i1, ci // PEER_TOPK, axis=-1)
    e2 = jnp.take_along_axis(i2, ci % PEER_TOPK, axis=-1)
    eid = (e1 * N_KEYS + e2).reshape(T, PEER_HEADS * PEER_TOPK)
    gate = jax.nn.softmax(vs, axis=-1).reshape(T, PEER_HEADS * PEER_TOPK)
    nch = T // PEER_CHUNK

    def chunk(args):
        hc, ec, gc = args
        a = jnp.einsum('ced,cd->ce', expert_u[ec], hc).astype(jnp.float32)
        act = jax.nn.gelu(a, approximate=False) * gc
        return jnp.einsum('ce,ced->cd', act, expert_v[ec].astype(jnp.float32))

    y = lax.map(chunk, (ht.reshape(nch, PEER_CHUNK, D),
                        eid.reshape(nch, PEER_CHUNK, -1),
                        gate.reshape(nch, PEER_CHUNK, -1)))
    return y.reshape(B, S, D).astype(h.dtype)


def hybrid_layer(x, c, w_ada, b_ada, norm1_g, w_in, b_f, q_norm_a, k_norm_a, q_norm_b, k_norm_b,
                 rel_bias, w_o, norm2_g, w_pq, sub_keys_1, sub_keys_2, expert_u, expert_v):
    B, S, D = x.shape
    mod = jnp.dot(jax.nn.silu(c), w_ada) + b_ada
    sh1, sc1, g1, sh2, sc2, g2 = jnp.split(mod[:, None, :], N_MOD, axis=-1)

    h = rmsnorm(x, norm1_g) * (1.0 + sc1) + sh1
    proj = jnp.dot(h, w_in)
    bounds = [WIDTH_A, 2 * WIDTH_A, 3 * WIDTH_A, 3 * WIDTH_A + WIDTH_B,
              3 * WIDTH_A + 2 * WIDTH_B, 3 * WIDTH_A + 3 * WIDTH_B]
    qa, ka, va, qb, kb, vb, fz = jnp.split(proj, bounds, axis=-1)

    def heads(t, n):
        return t.reshape(B, S, n, HEAD_DIM).transpose(0, 2, 1, 3)

    qa = rmsnorm(heads(qa, HEADS_A), q_norm_a)
    ka = rmsnorm(heads(ka, HEADS_A), k_norm_a)
    va = heads(va, HEADS_A)
    qb = rmsnorm(heads(qb, HEADS_B), q_norm_b)
    kb = rmsnorm(heads(kb, HEADS_B), k_norm_b)
    vb = heads(vb, HEADS_B)
    log_f = jax.nn.log_sigmoid((fz + b_f).astype(jnp.float32)).transpose(0, 2, 1)

    out_a = dilated_attention(qa, ka, va, rel_bias)
    out_b = forgetting_attention(qb, kb, vb, log_f)
    mix = jnp.concatenate([out_a, out_b], axis=1).transpose(0, 2, 1, 3).reshape(B, S, MIX_WIDTH)
    x = x + g1 * jnp.dot(mix.astype(x.dtype), w_o)

    h2 = rmsnorm(x, norm2_g) * (1.0 + sc2) + sh2
    x = x + g2 * peer(h2, w_pq, sub_keys_1, sub_keys_2, expert_u, expert_v)
    return x


def setup_inputs(seed: int = 0) -> dict:
    key = jax.random.key(seed)
    ks = jax.random.split(key, 20)
    f32 = jnp.float32
    dsc = D_MODEL ** -0.5
    nrm = lambda k, shape: jax.random.normal(k, shape, f32)
    return {
        'x': nrm(ks[0], (BATCH, SEQ, D_MODEL)),
        'c': nrm(ks[1], (BATCH, D_MODEL)),
        'w_ada': nrm(ks[2], (DEPTH, D_MODEL, N_MOD * D_MODEL)) * (0.5 * dsc),
        'b_ada': nrm(ks[3], (DEPTH, N_MOD * D_MODEL)) * 0.02,
        'norm1_g': 1.0 + 0.02 * nrm(ks[4], (DEPTH, D_MODEL)),
        'w_in': nrm(ks[5], (DEPTH, D_MODEL, IN_COLS)) * dsc,
        'b_f': FGATE_BIAS_MEAN + 0.1 * nrm(ks[6], (DEPTH, HEADS_B)),
        'q_norm_a': 1.0 + 0.02 * nrm(ks[7], (DEPTH, HEAD_DIM)),
        'k_norm_a': 1.0 + 0.02 * nrm(ks[8], (DEPTH, HEAD_DIM)),
        'q_norm_b': 1.0 + 0.02 * nrm(ks[9], (DEPTH, HEAD_DIM)),
        'k_norm_b': 1.0 + 0.02 * nrm(ks[10], (DEPTH, HEAD_DIM)),
        'rel_bias': 0.5 * nrm(ks[11], (NUM_BUCKETS, HEADS_A)),
        'w_o': nrm(ks[12], (DEPTH, MIX_WIDTH, D_MODEL)) * (MIX_WIDTH ** -0.5),
        'norm2_g': 1.0 + 0.02 * nrm(ks[13], (DEPTH, D_MODEL)),
        'w_pq': nrm(ks[14], (DEPTH, D_MODEL, PEER_HEADS * PEER_QUERY_DIM)) * dsc,
        'sub_keys_1': nrm(ks[15], (DEPTH, N_KEYS, PEER_HALF)) * (PEER_HALF ** -0.5),
        'sub_keys_2': nrm(ks[16], (DEPTH, N_KEYS, PEER_HALF)) * (PEER_HALF ** -0.5),
        'expert_u': nrm(ks[17], (DEPTH, N_EXPERTS, D_MODEL)) * dsc,
        'expert_v': nrm(ks[18], (DEPTH, N_EXPERTS, D_MODEL)) * 0.5,
    }


def reference(x, c, w_ada, b_ada, norm1_g, w_in, b_f, q_norm_a, k_norm_a, q_norm_b, k_norm_b,
              rel_bias, w_o, norm2_g, w_pq, sub_keys_1, sub_keys_2, expert_u, expert_v):
    for l in range(DEPTH):
        x = hybrid_layer(x, c, w_ada[l], b_ada[l], norm1_g[l], w_in[l], b_f[l],
                         q_norm_a[l], k_norm_a[l], q_norm_b[l], k_norm_b[l], rel_bias,
                         w_o[l], norm2_g[l], w_pq[l], sub_keys_1[l], sub_keys_2[l],
                         expert_u[l], expert_v[l])
    return x
```

```python
import functools
import math

import jax
import jax.numpy as jnp
from jax import lax
from jax.experimental import pallas as pl
from jax.experimental.pallas import tpu as pltpu

F32 = jnp.float32
BF16 = jnp.bfloat16

HEAD_DIM = 128
DILATED_BRANCHES = ((128, 1), (512, 4), (2048, 16))
BLOCK_Q = 128
CHUNK = BLOCK_Q * 16
NUM_BUCKETS = 32
MAX_DISTANCE = 2048
PEER_HEADS = 8
N_KEYS = 128
PEER_TOPK = 16
NORM_EPS = 1e-6
NEG_INF = -1e30
VMEM_LIMIT = 56 * 1024 * 1024

_NT = (((1,), (1,)), ((), ()))
_TN = (((0,), (0,)), ((), ()))


def _params(sem):
    return pltpu.CompilerParams(dimension_semantics=sem, vmem_limit_bytes=VMEM_LIMIT)


def _adaln_kernel(c_ref, w_ref, b_ref, o_ref):
    c = c_ref[...]
    s = c * jax.nn.sigmoid(c)
    o_ref[...] = jnp.dot(s.astype(BF16), w_ref[...].astype(BF16),
                         preferred_element_type=F32) + b_ref[...]


def _adaln(c_pad, w_ada, b_ada, tn=512):
    rows, d = c_pad.shape
    n = w_ada.shape[1]
    return pl.pallas_call(
        _adaln_kernel,
        grid=(n // tn,),
        in_specs=[pl.BlockSpec((rows, d), lambda j: (0, 0)),
                  pl.BlockSpec((d, tn), lambda j: (0, j)),
                  pl.BlockSpec((1, tn), lambda j: (0, j))],
        out_specs=pl.BlockSpec((rows, tn), lambda j: (0, j)),
        out_shape=jax.ShapeDtypeStruct((rows, n), F32),
        compiler_params=_params(("arbitrary",)),
        name="adaln",
    )(c_pad, w_ada, b_ada.reshape(1, n))


def _t5_bucket(dist):
    max_exact = NUM_BUCKETS // 2
    d32 = jnp.maximum(dist, 1).astype(F32)
    large = max_exact + (jnp.log(d32 / max_exact) / math.log(MAX_DISTANCE / max_exact)
                         * (NUM_BUCKETS - max_exact)).astype(jnp.int32)
    large = jnp.minimum(large, NUM_BUCKETS - 1)
    return jnp.where(dist < max_exact, dist, large)


def _bucket_tiles():
    tiles = []
    for window, dilation in DILATED_BRANCHES:
        nw = window // dilation
        rel = jnp.arange(BLOCK_Q)[:, None] + nw - jnp.arange(BLOCK_Q + nw)[None, :]
        in_win = (rel >= 0) & (rel <= nw)
        bucket = _t5_bucket(jnp.clip(rel, 0, nw) * dilation)
        tiles.append(jnp.where(in_win, bucket, -1).astype(jnp.int32))
    return jnp.stack(tiles)


def _bias_kernel(rb_ref, bucket_ref, o_ref):
    h = pl.program_id(1)
    bucket = bucket_ref[0]
    acc = jnp.full(bucket.shape, NEG_INF, F32)
    for b in range(NUM_BUCKETS):
        acc = jnp.where(bucket == b, rb_ref[b, h], acc)
    o_ref[0, 0] = acc


def _bias_tiles(rel_bias, heads_a):
    buckets = _bucket_tiles()
    nb, bq, bk = buckets.shape
    return pl.pallas_call(
        _bias_kernel,
        grid=(nb, heads_a),
        in_specs=[pl.BlockSpec(memory_space=pltpu.SMEM),
                  pl.BlockSpec((1, bq, bk), lambda d, h: (d, 0, 0))],
        out_specs=pl.BlockSpec((1, 1, bq, bk), lambda d, h: (d, h, 0, 0)),
        out_shape=jax.ShapeDtypeStruct((nb, heads_a, bq, bk), F32),
        compiler_params=_params(("arbitrary", "arbitrary")),
        name="bias_tiles",
    )(rel_bias, buckets)


def _split3(x):
    hi = x.astype(BF16)
    r = x - hi.astype(F32)
    mid = r.astype(BF16)
    lo = (r - mid.astype(F32)).astype(BF16)
    return hi, mid, lo


def _inproj_kernel(x_ref, mod_ref, g_ref, w_ref, wf_ref, bf_ref, gain_ref,
                   qkv_ref, f_ref, h_scr, carry_scr, *, tm, tn, seg_tiles):
    si = pl.program_id(1)
    j = pl.program_id(2)

    @pl.when(j == 0)
    def _():
        x = x_ref[0]
        ms = jnp.mean(x * x, axis=-1, keepdims=True)
        y = x * lax.rsqrt(ms + NORM_EPS) * g_ref[...]
        h = y * (1.0 + mod_ref[0, 1:2, :]) + mod_ref[0, 0:1, :]
        hb = h.astype(BF16)
        h_scr[...] = hb
        fz = jnp.dot(hb, wf_ref[...], preferred_element_type=F32) + bf_ref[...]
        lf = jnp.minimum(fz, 0.0) - jnp.log(1.0 + jnp.exp(-jnp.abs(fz)))

        @pl.when(si == 0)
        def _():
            carry_scr[...] = jnp.zeros_like(carry_scr)

        row = lax.broadcasted_iota(jnp.int32, (tm, tm), 0)
        col = lax.broadcasted_iota(jnp.int32, (tm, tm), 1)
        tri = jnp.where(col <= row, 1.0, 0.0).astype(BF16)
        hi, mid, lo = _split3(lf)
        cs = (jnp.dot(tri, hi, preferred_element_type=F32)
              + jnp.dot(tri, mid, preferred_element_type=F32)
              + jnp.dot(tri, lo, preferred_element_type=F32)) + carry_scr[0:1, :]
        f_ref[0] = cs
        carry_scr[0:1, :] = cs[tm - 1:tm, :]

    acc = jnp.dot(h_scr[...], w_ref[...], preferred_element_type=F32)
    seg = j // seg_tiles
    is_norm = jnp.logical_and(seg != 2, seg != 5)
    gain = gain_ref[0]
    for hh in range(tn // HEAD_DIM):
        y = acc[:, hh * HEAD_DIM:(hh + 1) * HEAD_DIM]
        ms = jnp.mean(y * y, axis=-1, keepdims=True)
        scale = jnp.where(is_norm, lax.rsqrt(ms + NORM_EPS), 1.0)
        qkv_ref[0, 0, hh] = (y * scale * gain).astype(BF16)


def _inproj(x, mod, norm1_g, w_qkv, w_f, b_f, gains, tm=512, tn=512):
    b, s, d = x.shape
    n = w_qkv.shape[1]
    width = n // 6
    hg = width // HEAD_DIM
    tn = min(tn, width)
    tm = min(tm, s)
    seg_tiles = width // tn
    hpt = tn // HEAD_DIM
    kern = functools.partial(_inproj_kernel, tm=tm, tn=tn, seg_tiles=seg_tiles)
    return pl.pallas_call(
        kern,
        grid=(b, s // tm, n // tn),
        in_specs=[pl.BlockSpec((1, tm, d), lambda bi, si, j: (bi, si, 0)),
                  pl.BlockSpec((1, 6, d), lambda bi, si, j: (bi, 0, 0)),
                  pl.BlockSpec((1, d), lambda bi, si, j: (0, 0)),
                  pl.BlockSpec((d, tn), lambda bi, si, j: (0, j)),
                  pl.BlockSpec((d, HEAD_DIM), lambda bi, si, j: (0, 0)),
                  pl.BlockSpec((1, HEAD_DIM), lambda bi, si, j: (0, 0)),
                  pl.BlockSpec((1, 1, HEAD_DIM), lambda bi, si, j: (j // seg_tiles, 0, 0))],
        out_specs=[pl.BlockSpec((1, 1, hpt, tm, HEAD_DIM),
                                lambda bi, si, j: (j // seg_tiles, bi, j % seg_tiles, si, 0)),
                   pl.BlockSpec((1, tm, HEAD_DIM), lambda bi, si, j: (bi, si, 0))],
        out_shape=[jax.ShapeDtypeStruct((6, b, hg, s, HEAD_DIM), BF16),
                   jax.ShapeDtypeStruct((b, s, HEAD_DIM), F32)],
        scratch_shapes=[pltpu.VMEM((tm, d), BF16), pltpu.VMEM((8, HEAD_DIM), F32)],
        compiler_params=_params(("arbitrary", "arbitrary", "arbitrary")),
        name="inproj",
    )(x, mod, norm1_g, w_qkv, w_f, b_f, gains)


def _dilated_kernel(*refs):
    nb = len(DILATED_BRANCHES)
    q_refs = refs[0:nb]
    kc_refs = refs[nb:2 * nb]
    kp_refs = refs[2 * nb:3 * nb]
    vc_refs = refs[3 * nb:4 * nb]
    vp_refs = refs[4 * nb:5 * nb]
    bias_ref = refs[5 * nb]
    o_ref = refs[5 * nb + 1]
    scr = refs[5 * nb + 2:]
    kf_scr = scr[0:nb]
    vf_scr = scr[nb:2 * nb]
    num_scr = scr[2 * nb:3 * nb - 1]
    m_scr = scr[3 * nb - 1:4 * nb - 2]
    den_scr = scr[4 * nb - 2:5 * nb - 3]
    first_chunk = pl.program_id(2) == 0
    bq = BLOCK_Q
    d_home = DILATED_BRANCHES[-1][1]

    def block(bi, r, row0, first):
        lanes = slice(r * HEAD_DIM, (r + 1) * HEAD_DIM)
        q = q_refs[bi][0, 0, 0, pl.ds(row0, bq), lanes]
        kw = kf_scr[bi][pl.ds(row0, 2 * bq), lanes]
        vw = vf_scr[bi][pl.ds(row0, 2 * bq), lanes]
        s = lax.dot_general(q, kw, _NT, preferred_element_type=F32)
        logits = s + bias_ref[bi, 0]
        if first:
            col = lax.broadcasted_iota(jnp.int32, logits.shape, 1)
            logits = jnp.where(col < jnp.where(first_chunk, bq, 0), NEG_INF, logits)
        m = jnp.max(logits, axis=-1, keepdims=True)
        p = jnp.exp(logits - m)
        den = jnp.sum(p, axis=-1, keepdims=True)
        num = jnp.dot(p.astype(BF16), vw, preferred_element_type=F32)
        return num, m, den

    for bi in range(nb):
        kf_scr[bi][0:bq, :] = kp_refs[bi][0, 0, 0]
        kf_scr[bi][bq:, :] = kc_refs[bi][0, 0, 0]
        vf_scr[bi][0:bq, :] = vp_refs[bi][0, 0, 0]
        vf_scr[bi][bq:, :] = vc_refs[bi][0, 0, 0]

    for bi in range(nb - 1):
        d = DILATED_BRANCHES[bi][1]
        rows = CHUNK // d
        nblk = rows // bq
        for r in range(d):
            base = r * rows

            def store(row0, res, bi=bi, base=base):
                num, m, den = res
                num_scr[bi][pl.ds(base + row0, bq), :] = num
                m_scr[bi][pl.ds(base + row0, bq), :] = m
                den_scr[bi][pl.ds(base + row0, bq), :] = den

            store(0, block(bi, r, 0, True))

            def body(n, carry, bi=bi, r=r, store=store):
                row0 = pl.multiple_of(n * bq, bq)
                store(row0, block(bi, r, row0, False))
                return carry

            lax.fori_loop(1, nblk, body, 0)

    for r in range(d_home):
        num, m, den = block(nb - 1, r, 0, True)
        parts = [(num, m, den)]
        for bi in range(nb - 1):
            d = DILATED_BRANCHES[bi][1]
            rows = CHUNK // d
            sel = pl.ds((r % d) * rows + r // d, bq, stride=d_home // d)
            parts.append((num_scr[bi][sel, :], m_scr[bi][sel, :], den_scr[bi][sel, :]))
        m_all = parts[0][1]
        for _, mb, _ in parts[1:]:
            m_all = jnp.maximum(m_all, mb)
        num_t = None
        den_t = None
        for nb_, mb, db in parts:
            w = jnp.exp(mb - m_all)
            num_t = nb_ * w if num_t is None else num_t + nb_ * w
            den_t = db * w if den_t is None else den_t + db * w
        o_ref[0, 0, :, r * HEAD_DIM:(r + 1) * HEAD_DIM] = (num_t / den_t).astype(o_ref.dtype)


def _dilated(qkv, bias):
    _, b, ha, s, _ = qkv.shape
    nchunk = s // CHUNK
    d_home = DILATED_BRANCHES[-1][1]
    views, q_specs, kc_specs, kp_specs, vc_specs, vp_specs = [], [], [], [], [], []
    scratch_kv, scratch_res = [], []
    for window, d in DILATED_BRANCHES:
        assert window // d == BLOCK_Q
        rows = CHUNK // d
        views.append(qkv.reshape(6, b, ha, s // d, d * HEAD_DIM))
        blk = (1, 1, 1, rows, d * HEAD_DIM)
        pblk = (1, 1, 1, BLOCK_Q, d * HEAD_DIM)
        prev = rows // BLOCK_Q

        def cur_map(seg):
            return lambda bi, h, c: (seg, bi, h, c, 0)

        def prev_map(seg, prev=prev):
            return lambda bi, h, c: (seg, bi, h, jnp.maximum(c * prev - 1, 0), 0)

        q_specs.append(pl.BlockSpec(blk, cur_map(0)))
        kc_specs.append(pl.BlockSpec(blk, cur_map(1)))
        kp_specs.append(pl.BlockSpec(pblk, prev_map(1)))
        vc_specs.append(pl.BlockSpec(blk, cur_map(2)))
        vp_specs.append(pl.BlockSpec(pblk, prev_map(2)))
        scratch_kv.append(pltpu.VMEM((rows + BLOCK_Q, d * HEAD_DIM), BF16))
    nb = len(DILATED_BRANCHES)
    scratch = scratch_kv + scratch_kv
    scratch += [pltpu.VMEM((CHUNK, HEAD_DIM), F32)] * (nb - 1)
    scratch += [pltpu.VMEM((CHUNK, 1), F32)] * (2 * (nb - 1))
    bias_spec = pl.BlockSpec((nb, 1, BLOCK_Q, 2 * BLOCK_Q), lambda bi, h, c: (0, h, 0, 0))
    out = pl.pallas_call(
        _dilated_kernel,
        grid=(b, ha, nchunk),
        in_specs=q_specs + kc_specs + kp_specs + vc_specs + vp_specs + [bias_spec],
        out_specs=pl.BlockSpec((1, 1, BLOCK_Q, d_home * HEAD_DIM), lambda bi, h, c: (bi, h, c, 0)),
        out_shape=jax.ShapeDtypeStruct((b, ha, s // d_home, d_home * HEAD_DIM), BF16),
        scratch_shapes=scratch,
        compiler_params=_params(("arbitrary", "arbitrary", "arbitrary")),
        name="dilated",
    )(*(views * 5), bias)
    return out.reshape(b, ha, s, HEAD_DIM)


def _fox_kernel(q_ref, k_ref, v_ref, fq_ref, fk_ref, o_ref, *, tq):
    h = pl.program_id(1)
    qi = pl.program_id(2)
    q = q_ref[0, 0, 0]
    ftile = fq_ref[0]
    lane = lax.broadcasted_iota(jnp.int32, ftile.shape, 1)
    fq = jnp.sum(jnp.where(lane == h, ftile, 0.0), axis=-1, keepdims=True)

    def step(j, carry, masked):
        m, l, acc = carry
        k0 = pl.multiple_of(j * tq, tq)
        kj = k_ref[0, 0, 0, pl.ds(k0, tq), :]
        vj = v_ref[0, 0, 0, pl.ds(k0, tq), :]
        s = lax.dot_general(q, kj, _NT, preferred_element_type=F32)
        t = s - fk_ref[0, 0, :, pl.ds(k0, tq)]
        if masked:
            row = lax.broadcasted_iota(jnp.int32, t.shape, 0)
            col = lax.broadcasted_iota(jnp.int32, t.shape, 1)
            t = jnp.where(col <= row, t, NEG_INF)
        m_new = jnp.maximum(m, fq + jnp.max(t, axis=-1, keepdims=True))
        alpha = jnp.exp(m - m_new)
        p = jnp.exp(t - (m_new - fq))
        l = alpha * l + jnp.sum(p, axis=-1, keepdims=True)
        acc = alpha * acc + jnp.dot(p.astype(BF16), vj, preferred_element_type=F32)
        return m_new, l, acc

    init = (jnp.full((tq, 1), NEG_INF, F32), jnp.zeros((tq, 1), F32),
            jnp.zeros((tq, HEAD_DIM), F32))
    carry = lax.fori_loop(0, qi, functools.partial(step, masked=False), init)
    m, l, acc = step(qi, carry, True)
    o_ref[0, 0] = (acc / l).astype(o_ref.dtype)


def _fox(qkv, f_cum, f_cum_t, tq=256):
    _, b, hb, s, _ = qkv.shape
    tq = min(tq, s)
    kern = functools.partial(_fox_kernel, tq=tq)
    return pl.pallas_call(
        kern,
        grid=(b, hb, s // tq),
        in_specs=[pl.BlockSpec((1, 1, 1, tq, HEAD_DIM), lambda bi, h, qi: (3, bi, h, qi, 0)),
                  pl.BlockSpec((1, 1, 1, s, HEAD_DIM), lambda bi, h, qi: (4, bi, h, 0, 0)),
                  pl.BlockSpec((1, 1, 1, s, HEAD_DIM), lambda bi, h, qi: (5, bi, h, 0, 0)),
                  pl.BlockSpec((1, tq, HEAD_DIM), lambda bi, h, qi: (bi, qi, 0)),
                  pl.BlockSpec((1, 1, 1, s), lambda bi, h, qi: (bi, h, 0, 0))],
        out_specs=pl.BlockSpec((1, 1, tq, HEAD_DIM), lambda bi, h, qi: (bi, h, qi, 0)),
        out_shape=jax.ShapeDtypeStruct((b, hb, s, HEAD_DIM), BF16),
        compiler_params=_params(("arbitrary", "arbitrary", "arbitrary")),
        name="fox",
    )(qkv, qkv, qkv, f_cum, f_cum_t)


def _outproj_kernel(oa_ref, ob_ref, w_ref, x_ref, mod_ref, o_ref, mix_scr):
    j = pl.program_id(2)
    ha = oa_ref.shape[1]
    hb = ob_ref.shape[1]

    @pl.when(j == 0)
    def _():
        for h in range(ha):
            mix_scr[:, h * HEAD_DIM:(h + 1) * HEAD_DIM] = oa_ref[0, h]
        for h in range(hb):
            mix_scr[:, (ha + h) * HEAD_DIM:(ha + h + 1) * HEAD_DIM] = ob_ref[0, h]

    acc = jnp.dot(mix_scr[...], w_ref[...], preferred_element_type=F32)
    o_ref[0] = x_ref[0] + mod_ref[0, 2:3, :] * acc


def _outproj(out_a, out_b, w_o, x, mod, tm=512, tn=512):
    b, s, d = x.shape
    ha, hb = out_a.shape[1], out_b.shape[1]
    tm = min(tm, s)
    tn = min(tn, d)
    return pl.pallas_call(
        _outproj_kernel,
        grid=(b, s // tm, d // tn),
        in_specs=[pl.BlockSpec((1, ha, tm, HEAD_DIM), lambda bi, si, j: (bi, 0, si, 0)),
                  pl.BlockSpec((1, hb, tm, HEAD_DIM), lambda bi, si, j: (bi, 0, si, 0)),
                  pl.BlockSpec((d, tn), lambda bi, si, j: (0, j)),
                  pl.BlockSpec((1, tm, tn), lambda bi, si, j: (bi, si, j)),
                  pl.BlockSpec((1, 6, tn), lambda bi, si, j: (bi, 0, j))],
        out_specs=pl.BlockSpec((1, tm, tn), lambda bi, si, j: (bi, si, j)),
        out_shape=jax.ShapeDtypeStruct((b, s, d), F32),
        scratch_shapes=[pltpu.VMEM((tm, d), BF16)],
        compiler_params=_params(("arbitrary", "arbitrary", "arbitrary")),
        name="outproj",
    )(out_a, out_b, w_o, x, mod)


def _top_desc(s, k):
    vals = []
    cur = s
    for _ in range(k):
        mx = jnp.max(cur, axis=0, keepdims=True)
        vals.append(mx)
        cur = jnp.where(cur >= mx, -jnp.inf, cur)
    return vals


def _peer_front_kernel(x_ref, mod_ref, g_ref, w_ref, sk1_ref, sk2_ref,
                       h2_ref, s1_ref, s2_ref, a1_ref, a2_ref, tau_ref, h_scr):
    hh = pl.program_id(2)

    @pl.when(hh == 0)
    def _():
        x = x_ref[0]
        ms = jnp.mean(x * x, axis=-1, keepdims=True)
        y = x * lax.rsqrt(ms + NORM_EPS) * g_ref[...]
        h = (y * (1.0 + mod_ref[0, 4:5, :]) + mod_ref[0, 3:4, :]).astype(BF16)
        h_scr[...] = h
        h2_ref[0] = h

    q = jnp.dot(h_scr[...], w_ref[...], preferred_element_type=F32)
    q1 = q[:, :N_KEYS].astype(BF16)
    q2 = q[:, N_KEYS:].astype(BF16)
    s1 = lax.dot_general(sk1_ref[...], q1, _NT, preferred_element_type=F32)
    s2 = lax.dot_general(sk2_ref[...], q2, _NT, preferred_element_type=F32)
    v1 = _top_desc(s1, PEER_TOPK)
    v2 = _top_desc(s2, PEER_TOPK)
    v1_all = jnp.concatenate(v1, axis=0)
    v2_all = jnp.concatenate(v2, axis=0)
    half = PEER_TOPK // 2
    cands = [v1[0] + v2_all]
    cands += [v1[a] + v2_all[0:half, :] for a in range(1, half)]
    cands += [v1_all[half:, :] + v2[0]]
    cand = jnp.concatenate(cands, axis=0)
    top = _top_desc(cand, PEER_TOPK)
    tau = top[-1]
    z = None
    for t in top:
        e = jnp.exp(t - top[0])
        z = e if z is None else z + e
    s1_ref[0, 0] = s1
    s2_ref[0, 0] = s2
    a1_ref[0, 0] = jnp.exp(s1 - v1[0]) / z
    a2_ref[0, 0] = jnp.exp(s2 - v2[0])
    tau_ref[0, 0] = tau


def _peer_front(x1, mod, norm2_g, w_pq, sk1, sk2, tm=256):
    b, s, d = x1.shape
    tm = min(tm, s)
    qd = 2 * N_KEYS
    tab = jax.ShapeDtypeStruct((b, PEER_HEADS, N_KEYS, s), F32)
    tab_spec = pl.BlockSpec((1, 1, N_KEYS, tm), lambda bi, si, hh: (bi, hh, 0, si))
    return pl.pallas_call(
        _peer_front_kernel,
        grid=(b, s // tm, PEER_HEADS),
        in_specs=[pl.BlockSpec((1, tm, d), lambda bi, si, hh: (bi, si, 0)),
                  pl.BlockSpec((1, 6, d), lambda bi, si, hh: (bi, 0, 0)),
                  pl.BlockSpec((1, d), lambda bi, si, hh: (0, 0)),
                  pl.BlockSpec((d, qd), lambda bi, si, hh: (0, hh)),
                  pl.BlockSpec((N_KEYS, N_KEYS), lambda bi, si, hh: (0, 0)),
                  pl.BlockSpec((N_KEYS, N_KEYS), lambda bi, si, hh: (0, 0))],
        out_specs=[pl.BlockSpec((1, tm, d), lambda bi, si, hh: (bi, si, 0)),
                   tab_spec, tab_spec, tab_spec, tab_spec,
                   pl.BlockSpec((1, 1, 1, tm), lambda bi, si, hh: (bi, hh, 0, si))],
        out_shape=[jax.ShapeDtypeStruct((b, s, d), BF16), tab, tab, tab, tab,
                   jax.ShapeDtypeStruct((b, PEER_HEADS, 1, s), F32)],
        scratch_shapes=[pltpu.VMEM((tm, d), BF16)],
        compiler_params=_params(("arbitrary", "arbitrary", "arbitrary")),
        name="peer_front",
    )(x1, mod, norm2_g, w_pq, sk1, sk2)


def _gelu(a):
    return 0.5 * a * (1.0 + lax.erf(a * (2.0 ** -0.5)))


def _peer_dense_kernel(h2_ref, u_ref, v_ref, s1_ref, s2_ref, a1_ref, a2_ref, tau_ref,
                       x_ref, mod_ref, o_ref, acc_scr, *, tt, eb):
    e = pl.program_id(2)
    ne = pl.num_programs(2)

    @pl.when(e == 0)
    def _():
        acc_scr[...] = jnp.zeros_like(acc_scr)

    a_t = lax.dot_general(u_ref[...], h2_ref[0], _NT, preferred_element_type=F32)
    g_a = _gelu(a_t)
    rows = []
    for il in range(eb // N_KEYS):
        i = e * (eb // N_KEYS) + il
        gate = jnp.zeros((N_KEYS, tt), F32)
        for hh in range(PEER_HEADS):
            c = s1_ref[0, hh, pl.ds(i, 1), :] + s2_ref[0, hh]
            w = a1_ref[0, hh, pl.ds(i, 1), :] * a2_ref[0, hh]
            gate = gate + jnp.where(c >= tau_ref[0, hh], w, 0.0)
        rows.append((g_a[il * N_KEYS:(il + 1) * N_KEYS, :] * gate).astype(BF16))
    act = jnp.concatenate(rows, axis=0)
    acc_scr[...] += lax.dot_general(act, v_ref[...], _TN, preferred_element_type=F32)

    @pl.when(e == ne - 1)
    def _():
        o_ref[0] = x_ref[0] + mod_ref[0, 5:6, :] * acc_scr[...]


def _peer_dense(h2, u, v, s1, s2, a1, a2, tau, x1, mod, tt=512, eb=512):
    b, s, d = x1.shape
    ne = u.shape[0]
    tt = min(tt, s)
    kern = functools.partial(_peer_dense_kernel, tt=tt, eb=eb)
    one = pl.Buffered(1)
    tab_spec = pl.BlockSpec((1, PEER_HEADS, N_KEYS, tt), lambda bi, ti, e: (bi, 0, 0, ti),
                            pipeline_mode=one)
    return pl.pallas_call(
        kern,
        grid=(b, s // tt, ne // eb),
        in_specs=[pl.BlockSpec((1, tt, d), lambda bi, ti, e: (bi, ti, 0), pipeline_mode=one),
                  pl.BlockSpec((eb, d), lambda bi, ti, e: (e, 0)),
                  pl.BlockSpec((eb, d), lambda bi, ti, e: (e, 0)),
                  tab_spec, tab_spec, tab_spec, tab_spec,
                  pl.BlockSpec((1, PEER_HEADS, 1, tt), lambda bi, ti, e: (bi, 0, 0, ti)),
                  pl.BlockSpec((1, tt, d), lambda bi, ti, e: (bi, ti, 0), pipeline_mode=one),
                  pl.BlockSpec((1, 6, d), lambda bi, ti, e: (bi, 0, 0))],
        out_specs=pl.BlockSpec((1, tt, d), lambda bi, ti, e: (bi, ti, 0), pipeline_mode=one),
        out_shape=jax.ShapeDtypeStruct((b, s, d), F32),
        scratch_shapes=[pltpu.VMEM((tt, d), F32)],
        compiler_params=_params(("arbitrary", "arbitrary", "arbitrary")),
        name="peer_dense",
    )(h2, u, v, s1, s2, a1, a2, tau, x1, mod)


def _layer(x, c, w_ada, b_ada, norm1_g, w_in, b_f, q_norm_a, k_norm_a, q_norm_b, k_norm_b,
           rel_bias, w_o, norm2_g, w_pq, sk1, sk2, expert_u, expert_v):
    b, s, d = x.shape
    heads = d // HEAD_DIM
    ha = heads // 2
    hb = heads - ha
    assert ha == hb and s % CHUNK == 0
    width = ha * HEAD_DIM

    c_pad = jnp.zeros((8, d), F32).at[:b].set(c)
    mod = _adaln(c_pad, w_ada, b_ada)[:b].reshape(b, 6, d)

    w_qkv = w_in[:, :6 * width].astype(BF16)
    w_f = jnp.zeros((d, HEAD_DIM), BF16).at[:, :hb].set(w_in[:, 6 * width:].astype(BF16))
    b_f_pad = jnp.zeros((1, HEAD_DIM), F32).at[0, :hb].set(b_f)
    scale = HEAD_DIM ** -0.5
    ones = jnp.ones((HEAD_DIM,), F32)
    gains = jnp.stack([q_norm_a * scale, k_norm_a, ones, q_norm_b * scale, k_norm_b, ones])
    qkv, f_cum = _inproj(x, mod, norm1_g.reshape(1, d), w_qkv, w_f, b_f_pad,
                         gains.reshape(6, 1, HEAD_DIM))

    bias = _bias_tiles(rel_bias, ha)
    out_a = _dilated(qkv, bias)
    f_cum_t = jnp.transpose(f_cum[:, :, :hb], (0, 2, 1)).reshape(b, hb, 1, s)
    out_b = _fox(qkv, f_cum, f_cum_t)
    x1 = _outproj(out_a, out_b, w_o.astype(BF16), x, mod)

    h2, s1, s2, a1, a2, tau = _peer_front(x1, mod, norm2_g.reshape(1, d), w_pq.astype(BF16),
                                          sk1.astype(BF16), sk2.astype(BF16))
    return _peer_dense(h2, expert_u.astype(BF16), expert_v.astype(BF16),
                       s1, s2, a1, a2, tau, x1, mod)


def kernel(x, c, w_ada, b_ada, norm1_g, w_in, b_f, q_norm_a, k_norm_a, q_norm_b, k_norm_b, rel_bias,
           w_o, norm2_g, w_pq, sub_keys_1, sub_keys_2, expert_u, expert_v):
    for l in range(w_ada.shape[0]):
        x = _layer(x, c, w_ada[l], b_ada[l], norm1_g[l], w_in[l], b_f[l], q_norm_a[l], k_norm_a[l],
                   q_norm_b[l], k_norm_b[l], rel_bias, w_o[l], norm2_g[l], w_pq[l],
                   sub_keys_1[l], sub_keys_2[l], expert_u[l], expert_v[l])
    return x
```

```python
import functools
import math

import jax
import jax.numpy as jnp
from jax import lax
from jax.experimental import pallas as pl
from jax.experimental.pallas import tpu as pltpu

F32 = jnp.float32
BF16 = jnp.bfloat16

HEAD_DIM = 128
DILATED_BRANCHES = ((128, 1), (512, 4), (2048, 16))
BLOCK_Q = 128
CHUNK = BLOCK_Q * 16
NUM_BUCKETS = 32
MAX_DISTANCE = 2048
PEER_HEADS = 8
N_KEYS = 128
PEER_TOPK = 16
NORM_EPS = 1e-6
NEG_INF = -1e30
LOG2E = 1.4426950408889634
VMEM_LIMIT = 62 * 1024 * 1024

_NT = (((1,), (1,)), ((), ()))
_TN = (((0,), (0,)), ((), ()))


def _params(sem):
    return pltpu.CompilerParams(dimension_semantics=sem, vmem_limit_bytes=VMEM_LIMIT)


def _adaln_kernel(c_ref, w_ref, b_ref, o_ref):
    c = c_ref[...]
    s = c * jax.nn.sigmoid(c)
    o_ref[...] = jnp.dot(s.astype(BF16), w_ref[...].astype(BF16),
                         preferred_element_type=F32) + b_ref[...]


def _adaln(c_pad, w_ada, b_ada, tn=512):
    rows, d = c_pad.shape
    n = w_ada.shape[1]
    return pl.pallas_call(
        _adaln_kernel,
        grid=(n // tn,),
        in_specs=[pl.BlockSpec((rows, d), lambda j: (0, 0)),
                  pl.BlockSpec((d, tn), lambda j: (0, j)),
                  pl.BlockSpec((1, tn), lambda j: (0, j))],
        out_specs=pl.BlockSpec((rows, tn), lambda j: (0, j)),
        out_shape=jax.ShapeDtypeStruct((rows, n), F32),
        compiler_params=_params(("arbitrary",)),
        name="adaln",
    )(c_pad, w_ada, b_ada.reshape(1, n))


def _t5_bucket(dist):
    max_exact = NUM_BUCKETS // 2
    d32 = jnp.maximum(dist, 1).astype(F32)
    large = max_exact + (jnp.log(d32 / max_exact) / math.log(MAX_DISTANCE / max_exact)
                         * (NUM_BUCKETS - max_exact)).astype(jnp.int32)
    large = jnp.minimum(large, NUM_BUCKETS - 1)
    return jnp.where(dist < max_exact, dist, large)


def _bucket_tiles():
    tiles = []
    for window, dilation in DILATED_BRANCHES:
        nw = window // dilation
        rel = jnp.arange(BLOCK_Q)[:, None] + nw - jnp.arange(BLOCK_Q + nw)[None, :]
        in_win = (rel >= 0) & (rel <= nw)
        bucket = _t5_bucket(jnp.clip(rel, 0, nw) * dilation)
        tiles.append(jnp.where(in_win, bucket, -1).astype(jnp.int32))
    return jnp.stack(tiles)


def _bias_kernel(rb_ref, bucket_ref, o_ref):
    h = pl.program_id(1)
    bucket = bucket_ref[0]
    acc = jnp.full(bucket.shape, NEG_INF, F32)
    for b in range(NUM_BUCKETS):
        acc = jnp.where(bucket == b, rb_ref[b, h], acc)
    o_ref[0, 0] = acc


def _bias_tiles(rel_bias, heads_a):
    buckets = _bucket_tiles()
    nb, bq, bk = buckets.shape
    return pl.pallas_call(
        _bias_kernel,
        grid=(nb, heads_a),
        in_specs=[pl.BlockSpec(memory_space=pltpu.SMEM),
                  pl.BlockSpec((1, bq, bk), lambda d, h: (d, 0, 0))],
        out_specs=pl.BlockSpec((1, 1, bq, bk), lambda d, h: (d, h, 0, 0)),
        out_shape=jax.ShapeDtypeStruct((nb, heads_a, bq, bk), F32),
        compiler_params=_params(("arbitrary", "arbitrary")),
        name="bias_tiles",
    )(rel_bias, buckets)


def _split3(x):
    hi = x.astype(BF16)
    r = x - hi.astype(F32)
    mid = r.astype(BF16)
    lo = (r - mid.astype(F32)).astype(BF16)
    return hi, mid, lo


def _inproj_kernel(x_ref, mod_ref, g_ref, w_ref, wf_ref, bf_ref, gain_ref,
                   qkv_ref, f_ref, *rest, tm, tn, seg_tiles):
    dils = [d for _, d in DILATED_BRANCHES if d > 1]
    view_refs = rest[:len(dils)]
    h_scr, carry_scr, y_scr = rest[len(dils):]
    si = pl.program_id(1)
    j = pl.program_id(2)

    @pl.when(j == 0)
    def _():
        @pl.when(si == 0)
        def _():
            carry_scr[...] = jnp.zeros_like(carry_scr)

        rc = min(tm, 256)
        row = lax.broadcasted_iota(jnp.int32, (rc, rc), 0)
        col = lax.broadcasted_iota(jnp.int32, (rc, rc), 1)
        tri = jnp.where(col <= row, 1.0, 0.0).astype(BF16)

        def chunk(ci, carry):
            r0 = pl.multiple_of(ci * rc, rc)
            x = x_ref[0, pl.ds(r0, rc), :]
            ms = jnp.mean(x * x, axis=-1, keepdims=True)
            y = x * lax.rsqrt(ms + NORM_EPS) * g_ref[...]
            h = y * (1.0 + mod_ref[0, 1:2, :]) + mod_ref[0, 0:1, :]
            hb = h.astype(BF16)
            h_scr[pl.ds(r0, rc), :] = hb
            fz = jnp.dot(hb, wf_ref[...], preferred_element_type=F32) + bf_ref[...]
            lf = jnp.minimum(fz, 0.0) - jnp.log(1.0 + jnp.exp(-jnp.abs(fz)))
            hi, mid, lo = _split3(lf)
            cs = (jnp.dot(tri, hi, preferred_element_type=F32)
                  + jnp.dot(tri, mid, preferred_element_type=F32)
                  + jnp.dot(tri, lo, preferred_element_type=F32)) + carry_scr[0:1, :]
            f_ref[0, pl.ds(r0, rc), :] = cs
            carry_scr[0:1, :] = cs[rc - 1:rc, :]
            return carry

        lax.fori_loop(0, tm // rc, chunk, 0)

    acc = jnp.dot(h_scr[...], w_ref[...], preferred_element_type=F32)
    seg = j // seg_tiles
    is_norm = jnp.logical_and(seg != 2, seg != 5)
    gain = gain_ref[0]
    for hh in range(tn // HEAD_DIM):
        y = acc[:, hh * HEAD_DIM:(hh + 1) * HEAD_DIM]
        ms = jnp.mean(y * y, axis=-1, keepdims=True)
        scale = jnp.where(is_norm, lax.rsqrt(ms + NORM_EPS), 1.0)
        yn = y * scale * gain
        qkv_ref[0, 0, hh] = yn.astype(BF16)
        y_scr[hh * tm:(hh + 1) * tm, :] = yn

    @pl.when(seg < 3)
    def _():
        for vref, d in zip(view_refs, dils):
            for hh in range(tn // HEAD_DIM):
                for r in range(d):
                    vref[0, 0, hh, :, r * HEAD_DIM:(r + 1) * HEAD_DIM] = (
                        y_scr[pl.ds(hh * tm + r, tm // d, stride=d), :].astype(BF16))


def _inproj(x, mod, norm1_g, w_qkv, w_f, b_f, gains, tm=1024, tn=512):
    b, s, d = x.shape
    n = w_qkv.shape[1]
    width = n // 6
    hg = width // HEAD_DIM
    tn = min(tn, width)
    tm = min(tm, s)
    seg_tiles = width // tn
    hpt = tn // HEAD_DIM
    kern = functools.partial(_inproj_kernel, tm=tm, tn=tn, seg_tiles=seg_tiles)
    last_a = 3 * seg_tiles - 1

    def view_map(bi, si, j):
        ja = jnp.minimum(j, last_a)
        return (ja // seg_tiles, bi, ja % seg_tiles, si, 0)

    dils = [d for _, d in DILATED_BRANCHES if d > 1]
    view_specs = [pl.BlockSpec((1, 1, hpt, tm // d, d * HEAD_DIM), view_map) for d in dils]
    view_shapes = [jax.ShapeDtypeStruct((3, b, hg, s // d, d * HEAD_DIM), BF16) for d in dils]
    return pl.pallas_call(
        kern,
        grid=(b, s // tm, n // tn),
        in_specs=[pl.BlockSpec((1, tm, d), lambda bi, si, j: (bi, si, 0),
                               pipeline_mode=pl.Buffered(1)),
                  pl.BlockSpec((1, 6, d), lambda bi, si, j: (bi, 0, 0)),
                  pl.BlockSpec((1, d), lambda bi, si, j: (0, 0)),
                  pl.BlockSpec((d, tn), lambda bi, si, j: (0, j)),
                  pl.BlockSpec((d, HEAD_DIM), lambda bi, si, j: (0, 0)),
                  pl.BlockSpec((1, HEAD_DIM), lambda bi, si, j: (0, 0)),
                  pl.BlockSpec((1, 1, HEAD_DIM), lambda bi, si, j: (j // seg_tiles, 0, 0))],
        out_specs=[pl.BlockSpec((1, 1, hpt, tm, HEAD_DIM),
                                lambda bi, si, j: (j // seg_tiles, bi, j % seg_tiles, si, 0)),
                   pl.BlockSpec((1, tm, HEAD_DIM), lambda bi, si, j: (bi, si, 0))] + view_specs,
        out_shape=[jax.ShapeDtypeStruct((6, b, hg, s, HEAD_DIM), BF16),
                   jax.ShapeDtypeStruct((b, s, HEAD_DIM), F32)] + view_shapes,
        scratch_shapes=[pltpu.VMEM((tm, d), BF16), pltpu.VMEM((8, HEAD_DIM), F32),
                        pltpu.VMEM((hpt * tm, HEAD_DIM), F32)],
        compiler_params=_params(("arbitrary", "arbitrary", "arbitrary")),
        name="inproj",
    )(x, mod, norm1_g, w_qkv, w_f, b_f, gains)


def _dilated_kernel(*refs):
    nb = len(DILATED_BRANCHES)
    q_refs, kc_refs, kp_refs, vc_refs, vp_refs = (refs[i * nb:(i + 1) * nb] for i in range(5))
    bias_ref = refs[5 * nb]
    o_ref = refs[5 * nb + 1]
    scr = refs[5 * nb + 2:]
    kf_scr, vf_scr, num_scr, m_scr, den_scr = (scr[i * nb:(i + 1) * nb] for i in range(5))
    first_chunk = pl.program_id(2) == 0
    bq = BLOCK_Q
    hd = HEAD_DIM
    bnt = (((2,), (2,)), ((0,), (0,)))
    bnn = (((2,), (1,)), ((0,), (0,)))

    for bi, (_, d) in enumerate(DILATED_BRANCHES):
        rows = CHUNK // d
        nblk = rows // bq
        kf_scr[bi][0:bq, :] = kp_refs[bi][0, 0, 0]
        kf_scr[bi][bq:, :] = kc_refs[bi][0, 0, 0]
        vf_scr[bi][0:bq, :] = vp_refs[bi][0, 0, 0]
        vf_scr[bi][bq:, :] = vc_refs[bi][0, 0, 0]
        blocks = [(r, n) for r in range(d) for n in range(nblk)]
        ng = len(blocks)
        q = jnp.stack([q_refs[bi][0, 0, 0, n * bq:(n + 1) * bq, r * hd:(r + 1) * hd] for r, n in blocks])
        kw = jnp.stack([kf_scr[bi][n * bq:(n + 2) * bq, r * hd:(r + 1) * hd] for r, n in blocks])
        vw = jnp.stack([vf_scr[bi][n * bq:(n + 2) * bq, r * hd:(r + 1) * hd] for r, n in blocks])
        s = lax.dot_general(q, kw, bnt, preferred_element_type=F32)
        gi = lax.broadcasted_iota(jnp.int32, (ng, 1, 2 * bq), 0)
        col = lax.broadcasted_iota(jnp.int32, (ng, 1, 2 * bq), 2)
        lim = jnp.where(gi % nblk == 0, jnp.where(first_chunk, bq, 0), 0)
        kill = jnp.where(col < lim, NEG_INF, 0.0)
        logits = s + bias_ref[bi, 0] + kill
        m = jnp.max(logits, axis=-1, keepdims=True)
        p = jnp.exp(logits - m)
        den = jnp.sum(p, axis=-1, keepdims=True)
        num = lax.dot_general(p.astype(BF16), vw, bnn, preferred_element_type=F32)
        for g, (r, n) in enumerate(blocks):
            sel = pl.ds(n * bq * d + r, bq, stride=d) if d > 1 else pl.ds(n * bq, bq)
            num_scr[bi][sel, :] = num[g]
            m_scr[bi][sel, :] = jnp.broadcast_to(m[g], (bq, hd))
            den_scr[bi][sel, :] = jnp.broadcast_to(den[g], (bq, hd))

    rc = 512
    for c in range(CHUNK // rc):
        rows = slice(c * rc, (c + 1) * rc)
        ms = [m_scr[bi][rows, :] for bi in range(nb)]
        m_all = functools.reduce(jnp.maximum, ms)
        num_t = None
        den_t = None
        for bi in range(nb):
            w = jnp.exp(ms[bi] - m_all)
            nw = num_scr[bi][rows, :] * w
            dw = den_scr[bi][rows, :] * w
            num_t = nw if num_t is None else num_t + nw
            den_t = dw if den_t is None else den_t + dw
        o_ref[0, 0, rows, :] = (num_t / den_t).astype(o_ref.dtype)


def _dilated(views, bias):
    _, b, ha, s, _ = views[0].shape
    nchunk = s // CHUNK
    q_specs, kc_specs, kp_specs, vc_specs, vp_specs = [], [], [], [], []
    scratch_kv = []
    for (window, d), view in zip(DILATED_BRANCHES, views):
        assert window // d == BLOCK_Q and view.shape[3:] == (s // d, d * HEAD_DIM)
        rows = CHUNK // d
        blk = (1, 1, 1, rows, d * HEAD_DIM)
        pblk = (1, 1, 1, BLOCK_Q, d * HEAD_DIM)
        prev = rows // BLOCK_Q

        def cur_map(seg):
            return lambda bi, h, c: (seg, bi, h, c, 0)

        def prev_map(seg, prev=prev):
            return lambda bi, h, c: (seg, bi, h, jnp.maximum(c * prev - 1, 0), 0)

        q_specs.append(pl.BlockSpec(blk, cur_map(0)))
        kc_specs.append(pl.BlockSpec(blk, cur_map(1)))
        kp_specs.append(pl.BlockSpec(pblk, prev_map(1)))
        vc_specs.append(pl.BlockSpec(blk, cur_map(2)))
        vp_specs.append(pl.BlockSpec(pblk, prev_map(2)))
        scratch_kv.append(pltpu.VMEM((rows + BLOCK_Q, d * HEAD_DIM), BF16))
    nb = len(DILATED_BRANCHES)
    scratch = scratch_kv + scratch_kv + [pltpu.VMEM((CHUNK, HEAD_DIM), F32)] * (3 * nb)
    bias_spec = pl.BlockSpec((nb, 1, BLOCK_Q, 2 * BLOCK_Q), lambda bi, h, c: (0, h, 0, 0))
    return pl.pallas_call(
        _dilated_kernel,
        grid=(b, ha, nchunk),
        in_specs=q_specs + kc_specs + kp_specs + vc_specs + vp_specs + [bias_spec],
        out_specs=pl.BlockSpec((1, 1, CHUNK, HEAD_DIM), lambda bi, h, c: (bi, h, c, 0)),
        out_shape=jax.ShapeDtypeStruct((b, ha, s, HEAD_DIM), BF16),
        scratch_shapes=scratch,
        compiler_params=_params(("arbitrary", "arbitrary", "arbitrary")),
        name="dilated",
    )(*(list(views) * 5), bias)


def _lane_cols(cols, n):
    lane = lax.broadcasted_iota(jnp.int32, (n, HEAD_DIM), 1)
    out = jnp.zeros((n, HEAD_DIM), F32)
    for c, col in enumerate(cols):
        out = jnp.where(lane == c, col, out)
    return out


def _fox_kernel(q_ref, k_ref, v_ref, f_ref, o_ref, ka_scr, va_scr, qa_scr, m_scr, acc_scr, *, tq):
    h = pl.program_id(1)
    qi = pl.program_id(2)
    s_len = k_ref.shape[3]
    hd = HEAD_DIM

    def f_col(r0, n):
        ftile = f_ref[0, pl.ds(r0, n), :]
        lane = lax.broadcasted_iota(jnp.int32, ftile.shape, 1)
        return jnp.sum(jnp.where(lane == h, ftile, 0.0), axis=-1, keepdims=True) * LOG2E

    def parts(x):
        return [p.astype(F32) for p in _split3(x)]

    @pl.when(qi == 0)
    def _():
        def build(ci, carry):
            r0 = pl.multiple_of(ci * tq, tq)
            ext = _lane_cols(parts(-f_col(r0, tq)) + [1.0] * 6, tq)
            ka_scr[pl.ds(r0, tq), 0:hd] = k_ref[0, 0, 0, pl.ds(r0, tq), :]
            ka_scr[pl.ds(r0, tq), hd:2 * hd] = ext.astype(BF16)
            va_scr[pl.ds(r0, tq), 0:hd] = v_ref[0, 0, 0, pl.ds(r0, tq), :]
            va_scr[pl.ds(r0, tq), hd:2 * hd] = _lane_cols([1.0], tq).astype(BF16)
            return carry

        lax.fori_loop(0, s_len // tq, build, 0)

    q0 = pl.multiple_of(qi * tq, tq)
    qcols = [1.0] * 3 + parts(f_col(q0, tq))
    qa_scr[:, 0:hd] = q_ref[0, 0, 0]
    qa_scr[:, hd:2 * hd] = _lane_cols(qcols, tq).astype(BF16)

    def scores(k0):
        return lax.dot_general(qa_scr[...], ka_scr[pl.ds(k0, tq), :], _NT,
                               preferred_element_type=F32)

    def diag_scores():
        row = lax.broadcasted_iota(jnp.int32, (tq, tq), 0)
        col = lax.broadcasted_iota(jnp.int32, (tq, tq), 1)
        return jnp.where(col <= row, scores(q0), NEG_INF)

    def fold_max(s):
        mr = m_scr[...]
        for c in range(tq // hd):
            mr = jnp.maximum(mr, s[:, c * hd:(c + 1) * hd])
        m_scr[...] = mr

    m_scr[...] = jnp.full(m_scr.shape, NEG_INF, F32)

    def sweep1(j, carry):
        fold_max(scores(pl.multiple_of(j * tq, tq)))
        return carry

    lax.fori_loop(0, qi, sweep1, 0)
    fold_max(diag_scores())
    m = jnp.max(m_scr[...], axis=-1, keepdims=True)
    qa_scr[:, hd:2 * hd] = _lane_cols(qcols + parts(-m), tq).astype(BF16)

    acc_scr[...] = jnp.zeros(acc_scr.shape, F32)

    def sweep2(j, carry):
        k0 = pl.multiple_of(j * tq, tq)
        p = jnp.exp2(scores(k0)).astype(BF16)
        acc_scr[...] += jnp.dot(p, va_scr[pl.ds(k0, tq), :], preferred_element_type=F32)
        return carry

    lax.fori_loop(0, qi, sweep2, 0)
    p = jnp.exp2(diag_scores()).astype(BF16)
    acc = acc_scr[...] + jnp.dot(p, va_scr[pl.ds(q0, tq), :], preferred_element_type=F32)
    o_ref[0, 0] = (acc[:, 0:hd] / acc[:, hd:hd + 1]).astype(o_ref.dtype)


def _fox(qkv, f_cum, tq=512):
    _, b, hb, s, _ = qkv.shape
    tq = min(tq, s)
    kern = functools.partial(_fox_kernel, tq=tq)
    return pl.pallas_call(
        kern,
        grid=(b, hb, s // tq),
        in_specs=[pl.BlockSpec((1, 1, 1, tq, HEAD_DIM), lambda bi, h, qi: (3, bi, h, qi, 0)),
                  pl.BlockSpec((1, 1, 1, s, HEAD_DIM), lambda bi, h, qi: (4, bi, h, 0, 0)),
                  pl.BlockSpec((1, 1, 1, s, HEAD_DIM), lambda bi, h, qi: (5, bi, h, 0, 0)),
                  pl.BlockSpec((1, s, HEAD_DIM), lambda bi, h, qi: (bi, 0, 0))],
        out_specs=pl.BlockSpec((1, 1, tq, HEAD_DIM), lambda bi, h, qi: (bi, h, qi, 0)),
        out_shape=jax.ShapeDtypeStruct((b, hb, s, HEAD_DIM), BF16),
        scratch_shapes=[pltpu.VMEM((s, 2 * HEAD_DIM), BF16), pltpu.VMEM((s, 2 * HEAD_DIM), BF16),
                        pltpu.VMEM((tq, 2 * HEAD_DIM), BF16), pltpu.VMEM((tq, HEAD_DIM), F32),
                        pltpu.VMEM((tq, 2 * HEAD_DIM), F32)],
        compiler_params=_params(("arbitrary", "arbitrary", "arbitrary")),
        name="fox",
    )(qkv, qkv, qkv, f_cum)


def _outproj_kernel(oa_ref, ob_ref, w_ref, x_ref, mod_ref, o_ref, mix_scr):
    j = pl.program_id(2)
    ha = oa_ref.shape[1]
    hb = ob_ref.shape[1]

    @pl.when(j == 0)
    def _():
        for h in range(ha):
            mix_scr[:, h * HEAD_DIM:(h + 1) * HEAD_DIM] = oa_ref[0, h]
        for h in range(hb):
            mix_scr[:, (ha + h) * HEAD_DIM:(ha + h + 1) * HEAD_DIM] = ob_ref[0, h]

    acc = jnp.dot(mix_scr[...], w_ref[...], preferred_element_type=F32)
    o_ref[0] = x_ref[0] + mod_ref[0, 2:3, :] * acc


def _outproj(out_a, out_b, w_o, x, mod, tm=512, tn=512):
    b, s, d = x.shape
    ha, hb = out_a.shape[1], out_b.shape[1]
    tm = min(tm, s)
    tn = min(tn, d)
    return pl.pallas_call(
        _outproj_kernel,
        grid=(b, s // tm, d // tn),
        in_specs=[pl.BlockSpec((1, ha, tm, HEAD_DIM), lambda bi, si, j: (bi, 0, si, 0)),
                  pl.BlockSpec((1, hb, tm, HEAD_DIM), lambda bi, si, j: (bi, 0, si, 0)),
                  pl.BlockSpec((d, tn), lambda bi, si, j: (0, j)),
                  pl.BlockSpec((1, tm, tn), lambda bi, si, j: (bi, si, j)),
                  pl.BlockSpec((1, 6, tn), lambda bi, si, j: (bi, 0, j))],
        out_specs=pl.BlockSpec((1, tm, tn), lambda bi, si, j: (bi, si, j)),
        out_shape=jax.ShapeDtypeStruct((b, s, d), F32),
        scratch_shapes=[pltpu.VMEM((tm, d), BF16)],
        compiler_params=_params(("arbitrary", "arbitrary", "arbitrary")),
        name="outproj",
    )(out_a, out_b, w_o, x, mod)


def _top_desc(s, k):
    vals = []
    cur = s
    for _ in range(k):
        mx = jnp.max(cur, axis=0, keepdims=True)
        vals.append(mx)
        cur = jnp.where(cur >= mx, -jnp.inf, cur)
    return vals


def _peer_front_kernel(x_ref, mod_ref, g_ref, w_ref, sk1_ref, sk2_ref,
                       h2_ref, s1_ref, s2_ref, a1_ref, a2_ref, tau_ref, h_scr):
    hh = pl.program_id(2)

    @pl.when(hh == 0)
    def _():
        x = x_ref[0]
        ms = jnp.mean(x * x, axis=-1, keepdims=True)
        y = x * lax.rsqrt(ms + NORM_EPS) * g_ref[...]
        h = (y * (1.0 + mod_ref[0, 4:5, :]) + mod_ref[0, 3:4, :]).astype(BF16)
        h_scr[...] = h
        h2_ref[0] = h

    q = jnp.dot(h_scr[...], w_ref[...], preferred_element_type=F32)
    q1 = q[:, :N_KEYS].astype(BF16)
    q2 = q[:, N_KEYS:].astype(BF16)
    s1 = lax.dot_general(sk1_ref[...], q1, _NT, preferred_element_type=F32)
    s2 = lax.dot_general(sk2_ref[...], q2, _NT, preferred_element_type=F32)
    v1 = _top_desc(s1, PEER_TOPK)
    v2 = _top_desc(s2, PEER_TOPK)
    v1_all = jnp.concatenate(v1, axis=0)
    v2_all = jnp.concatenate(v2, axis=0)
    half = PEER_TOPK // 2
    cands = [v1[0] + v2_all]
    cands += [v1[a] + v2_all[0:half, :] for a in range(1, half)]
    cands += [v1_all[half:, :] + v2[0]]
    cand = jnp.concatenate(cands, axis=0)
    top = _top_desc(cand, PEER_TOPK)
    tau = top[-1]
    z = None
    for t in top:
        e = jnp.exp(t - top[0])
        z = e if z is None else z + e
    s1_ref[0, 0] = s1
    s2_ref[0, 0] = s2
    a1_ref[0, 0] = jnp.exp(s1 - v1[0]) / z
    a2_ref[0, 0] = jnp.exp(s2 - v2[0])
    tau_ref[0, 0] = tau


def _peer_front(x1, mod, norm2_g, w_pq, sk1, sk2, tm=256):
    b, s, d = x1.shape
    tm = min(tm, s)
    qd = 2 * N_KEYS
    tab = jax.ShapeDtypeStruct((b, PEER_HEADS, N_KEYS, s), F32)
    tab_spec = pl.BlockSpec((1, 1, N_KEYS, tm), lambda bi, si, hh: (bi, hh, 0, si))
    return pl.pallas_call(
        _peer_front_kernel,
        grid=(b, s // tm, PEER_HEADS),
        in_specs=[pl.BlockSpec((1, tm, d), lambda bi, si, hh: (bi, si, 0)),
                  pl.BlockSpec((1, 6, d), lambda bi, si, hh: (bi, 0, 0)),
                  pl.BlockSpec((1, d), lambda bi, si, hh: (0, 0)),
                  pl.BlockSpec((d, qd), lambda bi, si, hh: (0, hh)),
                  pl.BlockSpec((N_KEYS, N_KEYS), lambda bi, si, hh: (0, 0)),
                  pl.BlockSpec((N_KEYS, N_KEYS), lambda bi, si, hh: (0, 0))],
        out_specs=[pl.BlockSpec((1, tm, d), lambda bi, si, hh: (bi, si, 0)),
                   tab_spec, tab_spec, tab_spec, tab_spec,
                   pl.BlockSpec((1, 1, 1, tm), lambda bi, si, hh: (bi, hh, 0, si))],
        out_shape=[jax.ShapeDtypeStruct((b, s, d), BF16), tab, tab, tab, tab,
                   jax.ShapeDtypeStruct((b, PEER_HEADS, 1, s), F32)],
        scratch_shapes=[pltpu.VMEM((tm, d), BF16)],
        compiler_params=_params(("arbitrary", "arbitrary", "arbitrary")),
        name="peer_front",
    )(x1, mod, norm2_g, w_pq, sk1, sk2)


def _gelu(a):
    return 0.5 * a * (1.0 + lax.erf(a * (2.0 ** -0.5)))


def _peer_dense_kernel(h2_ref, u_ref, vt_ref, s1_ref, s2_ref, a1_ref, a2_ref, tau_ref,
                       x_ref, mod_ref, o_ref, acc_scr, a_scr, act_scr, *, tt, eb, nblk):
    e = pl.program_id(2)
    rows_per_blk = eb // N_KEYS

    @pl.when(e == 0)
    def _():
        acc_scr[...] = jnp.zeros(acc_scr.shape, F32)
        a_scr[...] = jnp.zeros(a_scr.shape, F32)
        act_scr[...] = jnp.zeros(act_scr.shape, BF16)

    acc_scr[...] += jnp.dot(vt_ref[...], act_scr[...], preferred_element_type=F32)

    blk = jnp.clip(e - 1, 0, nblk - 1)
    for il in range(rows_per_blk):
        i = blk * rows_per_blk + il
        rows = slice(il * N_KEYS, (il + 1) * N_KEYS)
        s1_rows = [s1_ref[0, hh, pl.ds(i, 1), :] for hh in range(PEER_HEADS)]
        a1_rows = [a1_ref[0, hh, pl.ds(i, 1), :] for hh in range(PEER_HEADS)]
        for tg in range(tt // 128):
            lanes = slice(tg * 128, (tg + 1) * 128)
            gate = jnp.zeros((N_KEYS, 128), F32)
            for hh in range(PEER_HEADS):
                c = s1_rows[hh][:, lanes] + s2_ref[0, hh, :, lanes]
                w = a1_rows[hh][:, lanes] * a2_ref[0, hh, :, lanes]
                gate = gate + jnp.where(c >= tau_ref[0, hh, :, lanes], w, 0.0)
            act_scr[rows, lanes] = (_gelu(a_scr[rows, lanes]) * gate).astype(BF16)

    a_scr[...] = lax.dot_general(u_ref[...], h2_ref[0], _NT, preferred_element_type=F32)

    @pl.when(e == nblk + 1)
    def _():
        for c in range(acc_scr.shape[0] // tt):
            cols = slice(c * tt, (c + 1) * tt)
            o_ref[0, :, cols] = x_ref[0, :, cols] + mod_ref[0, 5:6, cols] * acc_scr[cols, :].T


def _peer_dense(h2, u, vt, s1, s2, a1, a2, tau, x1, mod, tt=512, eb=512):
    b, s, d = x1.shape
    nblk = u.shape[0] // eb
    tt = min(tt, s)
    kern = functools.partial(_peer_dense_kernel, tt=tt, eb=eb, nblk=nblk)
    one = pl.Buffered(1)
    tab_spec = pl.BlockSpec((1, PEER_HEADS, N_KEYS, tt), lambda bi, ti, e: (bi, 0, 0, ti),
                            pipeline_mode=one)
    return pl.pallas_call(
        kern,
        grid=(b, s // tt, nblk + 2),
        in_specs=[pl.BlockSpec((1, tt, d), lambda bi, ti, e: (bi, ti, 0), pipeline_mode=one),
                  pl.BlockSpec((eb, d), lambda bi, ti, e: (jnp.minimum(e, nblk - 1), 0)),
                  pl.BlockSpec((d, eb), lambda bi, ti, e: (0, jnp.clip(e - 2, 0, nblk - 1))),
                  tab_spec, tab_spec, tab_spec, tab_spec,
                  pl.BlockSpec((1, PEER_HEADS, 1, tt), lambda bi, ti, e: (bi, 0, 0, ti)),
                  pl.BlockSpec((1, tt, d), lambda bi, ti, e: (bi, ti, 0), pipeline_mode=one),
                  pl.BlockSpec((1, 6, d), lambda bi, ti, e: (bi, 0, 0))],
        out_specs=pl.BlockSpec((1, tt, d), lambda bi, ti, e: (bi, ti, 0), pipeline_mode=one),
        out_shape=jax.ShapeDtypeStruct((b, s, d), F32),
        scratch_shapes=[pltpu.VMEM((d, tt), F32), pltpu.VMEM((eb, tt), F32),
                        pltpu.VMEM((eb, tt), BF16)],
        compiler_params=_params(("arbitrary", "arbitrary", "arbitrary")),
        name="peer_dense",
    )(h2, u, vt, s1, s2, a1, a2, tau, x1, mod)


def _layer(x, c, w_ada, b_ada, norm1_g, w_in, b_f, q_norm_a, k_norm_a, q_norm_b, k_norm_b,
           rel_bias, w_o, norm2_g, w_pq, sk1, sk2, expert_u, expert_v):
    b, s, d = x.shape
    heads = d // HEAD_DIM
    ha = heads // 2
    hb = heads - ha
    assert ha == hb and s % CHUNK == 0
    width = ha * HEAD_DIM

    c_pad = jnp.zeros((8, d), F32).at[:b].set(c)
    mod = _adaln(c_pad, w_ada, b_ada)[:b].reshape(b, 6, d)

    w_qkv = w_in[:, :6 * width].astype(BF16)
    w_f = jnp.zeros((d, HEAD_DIM), BF16).at[:, :hb].set(w_in[:, 6 * width:].astype(BF16))
    b_f_pad = jnp.zeros((1, HEAD_DIM), F32).at[0, :hb].set(b_f)
    scale = HEAD_DIM ** -0.5
    ones = jnp.ones((HEAD_DIM,), F32)
    gains = jnp.stack([q_norm_a * scale, k_norm_a, ones,
                       q_norm_b * (scale * LOG2E), k_norm_b, ones])
    qkv, f_cum, *views = _inproj(x, mod, norm1_g.reshape(1, d), w_qkv, w_f, b_f_pad,
                                 gains.reshape(6, 1, HEAD_DIM))

    bias = _bias_tiles(rel_bias, ha)
    out_a = _dilated([qkv] + views, bias)
    out_b = _fox(qkv, f_cum)
    x1 = _outproj(out_a, out_b, w_o.astype(BF16), x, mod)

    h2, s1, s2, a1, a2, tau = _peer_front(x1, mod, norm2_g.reshape(1, d), w_pq.astype(BF16),
                                          sk1.astype(BF16), sk2.astype(BF16))
    return _peer_dense(h2, expert_u.astype(BF16), expert_v.T.astype(BF16),
                       s1, s2, a1, a2, tau, x1, mod)


def kernel(x, c, w_ada, b_ada, norm1_g, w_in, b_f, q_norm_a, k_norm_a, q_norm_b, k_norm_b, rel_bias,
           w_o, norm2_g, w_pq, sub_keys_1, sub_keys_2, expert_u, expert_v):
    for l in range(w_ada.shape[0]):
        x = _layer(x, c, w_ada[l], b_ada[l], norm1_g[l], w_in[l], b_f[l], q_norm_a[l], k_norm_a[l],
                   q_norm_b[l], k_norm_b[l], rel_bias, w_o[l], norm2_g[l], w_pq[l],
                   sub_keys_1[l], sub_keys_2[l], expert_u[l], expert_v[l])
    return x
```

```python
import functools
import math

import jax
import jax.numpy as jnp
from jax import lax
from jax.experimental import pallas as pl
from jax.experimental.pallas import tpu as pltpu

F32 = jnp.float32
BF16 = jnp.bfloat16

HEAD_DIM = 128
DILATED_BRANCHES = ((128, 1), (512, 4), (2048, 16))
BLOCK_Q = 128
CHUNK = BLOCK_Q * 16
NUM_BUCKETS = 32
MAX_DISTANCE = 2048
PEER_HEADS = 8
N_KEYS = 128
PEER_TOPK = 16
NORM_EPS = 1e-6
NEG_INF = -1e30
LOG2E = 1.4426950408889634
FOX_BOUND_MAX = 48.0
VMEM_LIMIT = 62 * 1024 * 1024

_NT = (((1,), (1,)), ((), ()))
_TN = (((0,), (0,)), ((), ()))


def _params(sem):
    return pltpu.CompilerParams(dimension_semantics=sem, vmem_limit_bytes=VMEM_LIMIT)


def _adaln_kernel(c_ref, w_ref, b_ref, o_ref):
    c = c_ref[...]
    s = c * jax.nn.sigmoid(c)
    o_ref[...] = jnp.dot(s.astype(BF16), w_ref[...].astype(BF16),
                         preferred_element_type=F32) + b_ref[...]


def _adaln(c_pad, w_ada, b_ada, tn=512):
    rows, d = c_pad.shape
    n = w_ada.shape[1]
    return pl.pallas_call(
        _adaln_kernel,
        grid=(n // tn,),
        in_specs=[pl.BlockSpec((rows, d), lambda j: (0, 0)),
                  pl.BlockSpec((d, tn), lambda j: (0, j)),
                  pl.BlockSpec((1, tn), lambda j: (0, j))],
        out_specs=pl.BlockSpec((rows, tn), lambda j: (0, j)),
        out_shape=jax.ShapeDtypeStruct((rows, n), F32),
        compiler_params=_params(("arbitrary",)),
        name="adaln",
    )(c_pad, w_ada, b_ada.reshape(1, n))


def _t5_bucket(dist):
    max_exact = NUM_BUCKETS // 2
    d32 = jnp.maximum(dist, 1).astype(F32)
    large = max_exact + (jnp.log(d32 / max_exact) / math.log(MAX_DISTANCE / max_exact)
                         * (NUM_BUCKETS - max_exact)).astype(jnp.int32)
    large = jnp.minimum(large, NUM_BUCKETS - 1)
    return jnp.where(dist < max_exact, dist, large)


def _bucket_tiles():
    tiles = []
    for window, dilation in DILATED_BRANCHES:
        nw = window // dilation
        rel = jnp.arange(BLOCK_Q)[:, None] + nw - jnp.arange(BLOCK_Q + nw)[None, :]
        in_win = (rel >= 0) & (rel <= nw)
        bucket = _t5_bucket(jnp.clip(rel, 0, nw) * dilation)
        tiles.append(jnp.where(in_win, bucket, -1).astype(jnp.int32))
    return jnp.stack(tiles)


def _bias_kernel(rb_ref, bucket_ref, o_ref):
    h = pl.program_id(1)
    bucket = bucket_ref[0]
    acc = jnp.full(bucket.shape, NEG_INF, F32)
    for b in range(NUM_BUCKETS):
        acc = jnp.where(bucket == b, rb_ref[b, h], acc)
    o_ref[0, 0] = acc


def _bias_tiles(rel_bias, heads_a):
    buckets = _bucket_tiles()
    nb, bq, bk = buckets.shape
    return pl.pallas_call(
        _bias_kernel,
        grid=(nb, heads_a),
        in_specs=[pl.BlockSpec(memory_space=pltpu.SMEM),
                  pl.BlockSpec((1, bq, bk), lambda d, h: (d, 0, 0))],
        out_specs=pl.BlockSpec((1, 1, bq, bk), lambda d, h: (d, h, 0, 0)),
        out_shape=jax.ShapeDtypeStruct((nb, heads_a, bq, bk), F32),
        compiler_params=_params(("arbitrary", "arbitrary")),
        name="bias_tiles",
    )(rel_bias, buckets)


def _split3(x):
    hi = x.astype(BF16)
    r = x - hi.astype(F32)
    mid = r.astype(BF16)
    lo = (r - mid.astype(F32)).astype(BF16)
    return hi, mid, lo


def _inproj_kernel(x_ref, mod_ref, g_ref, w_ref, wf_ref, bf_ref, gain_ref,
                   qkv_ref, f_ref, *rest, tm, tn, seg_tiles):
    dils = [d for _, d in DILATED_BRANCHES if d > 1]
    view_refs = rest[:len(dils)]
    h_scr, carry_scr, y_scr = rest[len(dils):]
    si = pl.program_id(1)
    j = pl.program_id(2)

    @pl.when(j == 0)
    def _():
        @pl.when(si == 0)
        def _():
            carry_scr[...] = jnp.zeros_like(carry_scr)

        rc = min(tm, 256)
        row = lax.broadcasted_iota(jnp.int32, (rc, rc), 0)
        col = lax.broadcasted_iota(jnp.int32, (rc, rc), 1)
        tri = jnp.where(col <= row, 1.0, 0.0).astype(BF16)

        def chunk(ci, carry):
            r0 = pl.multiple_of(ci * rc, rc)
            x = x_ref[0, pl.ds(r0, rc), :]
            ms = jnp.mean(x * x, axis=-1, keepdims=True)
            y = x * lax.rsqrt(ms + NORM_EPS) * g_ref[...]
            h = y * (1.0 + mod_ref[0, 1:2, :]) + mod_ref[0, 0:1, :]
            hb = h.astype(BF16)
            h_scr[pl.ds(r0, rc), :] = hb
            fz = jnp.dot(hb, wf_ref[...], preferred_element_type=F32) + bf_ref[...]
            lf = jnp.minimum(fz, 0.0) - jnp.log(1.0 + jnp.exp(-jnp.abs(fz)))
            hi, mid, lo = _split3(lf)
            cs = (jnp.dot(tri, hi, preferred_element_type=F32)
                  + jnp.dot(tri, mid, preferred_element_type=F32)
                  + jnp.dot(tri, lo, preferred_element_type=F32)) + carry_scr[0:1, :]
            f_ref[0, pl.ds(r0, rc), :] = cs
            carry_scr[0:1, :] = cs[rc - 1:rc, :]
            return carry

        lax.fori_loop(0, tm // rc, chunk, 0)

    acc = jnp.dot(h_scr[...], w_ref[...], preferred_element_type=F32)
    seg = j // seg_tiles
    is_norm = jnp.logical_and(seg != 2, seg != 5)
    gain = gain_ref[0]
    for hh in range(tn // HEAD_DIM):
        y = acc[:, hh * HEAD_DIM:(hh + 1) * HEAD_DIM]
        ms = jnp.mean(y * y, axis=-1, keepdims=True)
        scale = jnp.where(is_norm, lax.rsqrt(ms + NORM_EPS), 1.0)
        yn = y * scale * gain
        qkv_ref[0, 0, hh] = yn.astype(BF16)
        y_scr[hh * tm:(hh + 1) * tm, :] = yn

    @pl.when(seg < 3)
    def _():
        for vref, d in zip(view_refs, dils):
            for hh in range(tn // HEAD_DIM):
                for r in range(d):
                    vref[0, 0, hh, :, r * HEAD_DIM:(r + 1) * HEAD_DIM] = (
                        y_scr[pl.ds(hh * tm + r, tm // d, stride=d), :].astype(BF16))


def _inproj(x, mod, norm1_g, w_qkv, w_f, b_f, gains, tm=1024, tn=512):
    b, s, d = x.shape
    n = 3 * d
    width = n // 6
    hg = width // HEAD_DIM
    tn = min(tn, width)
    tm = min(tm, s)
    seg_tiles = width // tn
    hpt = tn // HEAD_DIM
    kern = functools.partial(_inproj_kernel, tm=tm, tn=tn, seg_tiles=seg_tiles)
    last_a = 3 * seg_tiles - 1

    def view_map(bi, si, j):
        ja = jnp.minimum(j, last_a)
        return (ja // seg_tiles, bi, ja % seg_tiles, si, 0)

    dils = [d for _, d in DILATED_BRANCHES if d > 1]
    view_specs = [pl.BlockSpec((1, 1, hpt, tm // d, d * HEAD_DIM), view_map) for d in dils]
    view_shapes = [jax.ShapeDtypeStruct((3, b, hg, s // d, d * HEAD_DIM), BF16) for d in dils]
    return pl.pallas_call(
        kern,
        grid=(b, s // tm, n // tn),
        in_specs=[pl.BlockSpec((1, tm, d), lambda bi, si, j: (bi, si, 0),
                               pipeline_mode=pl.Buffered(1)),
                  pl.BlockSpec((1, 6, d), lambda bi, si, j: (bi, 0, 0)),
                  pl.BlockSpec((1, d), lambda bi, si, j: (0, 0)),
                  pl.BlockSpec((d, tn), lambda bi, si, j: (0, j)),
                  pl.BlockSpec((d, HEAD_DIM), lambda bi, si, j: (0, 0)),
                  pl.BlockSpec((1, HEAD_DIM), lambda bi, si, j: (0, 0)),
                  pl.BlockSpec((1, 1, HEAD_DIM), lambda bi, si, j: (j // seg_tiles, 0, 0))],
        out_specs=[pl.BlockSpec((1, 1, hpt, tm, HEAD_DIM),
                                lambda bi, si, j: (j // seg_tiles, bi, j % seg_tiles, si, 0)),
                   pl.BlockSpec((1, tm, HEAD_DIM), lambda bi, si, j: (bi, si, 0))] + view_specs,
        out_shape=[jax.ShapeDtypeStruct((6, b, hg, s, HEAD_DIM), BF16),
                   jax.ShapeDtypeStruct((b, s, HEAD_DIM), F32)] + view_shapes,
        scratch_shapes=[pltpu.VMEM((tm, d), BF16), pltpu.VMEM((8, HEAD_DIM), F32),
                        pltpu.VMEM((hpt * tm, HEAD_DIM), F32)],
        compiler_params=_params(("arbitrary", "arbitrary", "arbitrary")),
        name="inproj",
    )(x, mod, norm1_g, w_qkv, w_f, b_f, gains)


def _dilated_kernel(*refs):
    nb = len(DILATED_BRANCHES)
    q_refs, kc_refs, kp_refs, vc_refs, vp_refs = (refs[i * nb:(i + 1) * nb] for i in range(5))
    bias_ref = refs[5 * nb]
    o_ref = refs[5 * nb + 1]
    scr = refs[5 * nb + 2:]
    kf_scr, vf_scr, num_scr, m_scr, den_scr = (scr[i * nb:(i + 1) * nb] for i in range(5))
    first_chunk = pl.program_id(2) == 0
    bq = BLOCK_Q
    hd = HEAD_DIM
    bnt = (((2,), (2,)), ((0,), (0,)))
    bnn = (((2,), (1,)), ((0,), (0,)))

    for bi, (_, d) in enumerate(DILATED_BRANCHES):
        rows = CHUNK // d
        nblk = rows // bq
        kf_scr[bi][0:bq, :] = kp_refs[bi][0, 0, 0]
        kf_scr[bi][bq:, :] = kc_refs[bi][0, 0, 0]
        vf_scr[bi][0:bq, :] = vp_refs[bi][0, 0, 0]
        vf_scr[bi][bq:, :] = vc_refs[bi][0, 0, 0]
        blocks = [(r, n) for r in range(d) for n in range(nblk)]
        ng = len(blocks)
        q = jnp.stack([q_refs[bi][0, 0, 0, n * bq:(n + 1) * bq, r * hd:(r + 1) * hd] for r, n in blocks])
        kw = jnp.stack([kf_scr[bi][n * bq:(n + 2) * bq, r * hd:(r + 1) * hd] for r, n in blocks])
        vw = jnp.stack([vf_scr[bi][n * bq:(n + 2) * bq, r * hd:(r + 1) * hd] for r, n in blocks])
        s = lax.dot_general(q, kw, bnt, preferred_element_type=F32)
        gi = lax.broadcasted_iota(jnp.int32, (ng, 1, 2 * bq), 0)
        col = lax.broadcasted_iota(jnp.int32, (ng, 1, 2 * bq), 2)
        lim = jnp.where(gi % nblk == 0, jnp.where(first_chunk, bq, 0), 0)
        kill = jnp.where(col < lim, NEG_INF, 0.0)
        logits = s + bias_ref[bi, 0] + kill
        m = jnp.max(logits, axis=-1, keepdims=True)
        p = jnp.exp(logits - m)
        den = jnp.sum(p, axis=-1, keepdims=True)
        num = lax.dot_general(p.astype(BF16), vw, bnn, preferred_element_type=F32)
        for g, (r, n) in enumerate(blocks):
            sel = pl.ds(n * bq * d + r, bq, stride=d) if d > 1 else pl.ds(n * bq, bq)
            num_scr[bi][sel, :] = num[g]
            m_scr[bi][sel, :] = jnp.broadcast_to(m[g], (bq, hd))
            den_scr[bi][sel, :] = jnp.broadcast_to(den[g], (bq, hd))

    rc = 512
    for c in range(CHUNK // rc):
        rows = slice(c * rc, (c + 1) * rc)
        ms = [m_scr[bi][rows, :] for bi in range(nb)]
        m_all = functools.reduce(jnp.maximum, ms)
        num_t = None
        den_t = None
        for bi in range(nb):
            w = jnp.exp(ms[bi] - m_all)
            nw = num_scr[bi][rows, :] * w
            dw = den_scr[bi][rows, :] * w
            num_t = nw if num_t is None else num_t + nw
            den_t = dw if den_t is None else den_t + dw
        o_ref[0, 0, rows, :] = (num_t / den_t).astype(o_ref.dtype)


def _dilated(views, bias):
    _, b, ha, s, _ = views[0].shape
    nchunk = s // CHUNK
    q_specs, kc_specs, kp_specs, vc_specs, vp_specs = [], [], [], [], []
    scratch_kv = []
    for (window, d), view in zip(DILATED_BRANCHES, views):
        assert window // d == BLOCK_Q and view.shape[3:] == (s // d, d * HEAD_DIM)
        rows = CHUNK // d
        blk = (1, 1, 1, rows, d * HEAD_DIM)
        pblk = (1, 1, 1, BLOCK_Q, d * HEAD_DIM)
        prev = rows // BLOCK_Q

        def cur_map(seg):
            return lambda bi, h, c: (seg, bi, h, c, 0)

        def prev_map(seg, prev=prev):
            return lambda bi, h, c: (seg, bi, h, jnp.maximum(c * prev - 1, 0), 0)

        q_specs.append(pl.BlockSpec(blk, cur_map(0)))
        kc_specs.append(pl.BlockSpec(blk, cur_map(1)))
        kp_specs.append(pl.BlockSpec(pblk, prev_map(1)))
        vc_specs.append(pl.BlockSpec(blk, cur_map(2)))
        vp_specs.append(pl.BlockSpec(pblk, prev_map(2)))
        scratch_kv.append(pltpu.VMEM((rows + BLOCK_Q, d * HEAD_DIM), BF16))
    nb = len(DILATED_BRANCHES)
    scratch = scratch_kv + scratch_kv + [pltpu.VMEM((CHUNK, HEAD_DIM), F32)] * (3 * nb)
    bias_spec = pl.BlockSpec((nb, 1, BLOCK_Q, 2 * BLOCK_Q), lambda bi, h, c: (0, h, 0, 0))
    return pl.pallas_call(
        _dilated_kernel,
        grid=(b, ha, nchunk),
        in_specs=q_specs + kc_specs + kp_specs + vc_specs + vp_specs + [bias_spec],
        out_specs=pl.BlockSpec((1, 1, CHUNK, HEAD_DIM), lambda bi, h, c: (bi, h, c, 0)),
        out_shape=jax.ShapeDtypeStruct((b, ha, s, HEAD_DIM), BF16),
        scratch_shapes=scratch,
        compiler_params=_params(("arbitrary", "arbitrary", "arbitrary")),
        name="dilated",
    )(*(list(views) * 5), bias)


def _lane_cols(cols, n):
    lane = lax.broadcasted_iota(jnp.int32, (n, HEAD_DIM), 1)
    out = jnp.zeros((n, HEAD_DIM), F32)
    for c, col in enumerate(cols):
        out = jnp.where(lane == c, col, out)
    return out


def _fox_kernel(q_ref, k_ref, v_ref, f_ref, o_ref, ka_scr, va_scr, qa_scr, m_scr, acc_scr, kn_scr,
                *, tq):
    h = pl.program_id(1)
    qi = pl.program_id(2)
    s_len = k_ref.shape[3]
    hd = HEAD_DIM

    def f_col(r0, n):
        ftile = f_ref[0, pl.ds(r0, n), :]
        lane = lax.broadcasted_iota(jnp.int32, ftile.shape, 1)
        return jnp.sum(jnp.where(lane == h, ftile, 0.0), axis=-1, keepdims=True) * LOG2E

    def parts(x):
        return [p.astype(F32) for p in _split3(x)]

    @pl.when(qi == 0)
    def _():
        kn_scr[...] = jnp.zeros(kn_scr.shape, F32)

        def build(ci, carry):
            r0 = pl.multiple_of(ci * tq, tq)
            ext = _lane_cols(parts(-f_col(r0, tq)) + [1.0] * 6, tq)
            k = k_ref[0, 0, 0, pl.ds(r0, tq), :]
            k32 = k.astype(F32)
            kn = jnp.max(jnp.sum(k32 * k32, axis=-1, keepdims=True), axis=0, keepdims=True)
            kn_scr[...] = jnp.maximum(kn_scr[...], jnp.broadcast_to(kn, kn_scr.shape))
            ka_scr[pl.ds(r0, tq), 0:hd] = k
            ka_scr[pl.ds(r0, tq), hd:2 * hd] = ext.astype(BF16)
            va_scr[pl.ds(r0, tq), 0:hd] = v_ref[0, 0, 0, pl.ds(r0, tq), :]
            va_scr[pl.ds(r0, tq), hd:2 * hd] = _lane_cols([1.0], tq).astype(BF16)
            return carry

        lax.fori_loop(0, s_len // tq, build, 0)

    q0 = pl.multiple_of(qi * tq, tq)
    qcols = [1.0] * 3 + parts(f_col(q0, tq))
    qa_scr[:, 0:hd] = q_ref[0, 0, 0]
    qa_scr[:, hd:2 * hd] = _lane_cols(qcols, tq).astype(BF16)

    def scores(k0, width):
        return lax.dot_general(qa_scr[...], ka_scr[pl.ds(k0, width), :], _NT,
                               preferred_element_type=F32)

    def diag_scores():
        row = lax.broadcasted_iota(jnp.int32, (tq, tq), 0)
        col = lax.broadcasted_iota(jnp.int32, (tq, tq), 1)
        return jnp.where(col <= row, scores(q0, tq), NEG_INF)

    def sweep(tile):
        def pair(j, carry):
            tile(pl.multiple_of(j * 2 * tq, 2 * tq), 2 * tq)
            return carry

        lax.fori_loop(0, qi // 2, pair, 0)

        @pl.when(qi % 2 == 1)
        def _():
            tile(pl.multiple_of((qi - 1) * tq, tq), tq)

    def fold_max(s):
        mr = m_scr[...]
        for c in range(s.shape[1] // hd):
            mr = jnp.maximum(mr, s[:, c * hd:(c + 1) * hd])
        m_scr[...] = mr

    q32 = q_ref[0, 0, 0].astype(F32)
    bound = jnp.sqrt(jnp.sum(q32 * q32, axis=-1, keepdims=True) * kn_scr[0:1, 0:1])
    in_range = jnp.max(bound) < FOX_BOUND_MAX

    @pl.when(in_range)
    def _():
        qa_scr[:, hd:2 * hd] = _lane_cols(qcols + parts(-bound), tq).astype(BF16)

    @pl.when(jnp.logical_not(in_range))
    def _():
        m_scr[...] = jnp.full(m_scr.shape, NEG_INF, F32)
        sweep(lambda k0, width: fold_max(scores(k0, width)))
        fold_max(diag_scores())
        m = jnp.max(m_scr[...], axis=-1, keepdims=True)
        qa_scr[:, hd:2 * hd] = _lane_cols(qcols + parts(-m), tq).astype(BF16)

    acc_scr[...] = jnp.zeros(acc_scr.shape, F32)

    def pv_tile(k0, width):
        p = jnp.exp2(scores(k0, width)).astype(BF16)
        acc_scr[...] += jnp.dot(p, va_scr[pl.ds(k0, width), :], preferred_element_type=F32)

    sweep(pv_tile)
    p = jnp.exp2(diag_scores()).astype(BF16)
    acc = acc_scr[...] + jnp.dot(p, va_scr[pl.ds(q0, tq), :], preferred_element_type=F32)
    o_ref[0, 0] = (acc[:, 0:hd] / acc[:, hd:hd + 1]).astype(o_ref.dtype)


def _fox(qkv, f_cum, tq=512):
    _, b, hb, s, _ = qkv.shape
    tq = min(tq, s)
    kern = functools.partial(_fox_kernel, tq=tq)
    return pl.pallas_call(
        kern,
        grid=(b, hb, s // tq),
        in_specs=[pl.BlockSpec((1, 1, 1, tq, HEAD_DIM), lambda bi, h, qi: (3, bi, h, qi, 0)),
                  pl.BlockSpec((1, 1, 1, s, HEAD_DIM), lambda bi, h, qi: (4, bi, h, 0, 0)),
                  pl.BlockSpec((1, 1, 1, s, HEAD_DIM), lambda bi, h, qi: (5, bi, h, 0, 0)),
                  pl.BlockSpec((1, s, HEAD_DIM), lambda bi, h, qi: (bi, 0, 0))],
        out_specs=pl.BlockSpec((1, 1, tq, HEAD_DIM), lambda bi, h, qi: (bi, h, qi, 0)),
        out_shape=jax.ShapeDtypeStruct((b, hb, s, HEAD_DIM), BF16),
        scratch_shapes=[pltpu.VMEM((s, 2 * HEAD_DIM), BF16), pltpu.VMEM((s, 2 * HEAD_DIM), BF16),
                        pltpu.VMEM((tq, 2 * HEAD_DIM), BF16), pltpu.VMEM((tq, HEAD_DIM), F32),
                        pltpu.VMEM((tq, 2 * HEAD_DIM), F32), pltpu.VMEM((8, HEAD_DIM), F32)],
        compiler_params=_params(("arbitrary", "arbitrary", "arbitrary")),
        name="fox",
    )(qkv, qkv, qkv, f_cum)


def _outproj_kernel(oa_ref, ob_ref, w_ref, x_ref, mod_ref, o_ref, mix_scr):
    j = pl.program_id(2)
    ha = oa_ref.shape[1]
    hb = ob_ref.shape[1]

    @pl.when(j == 0)
    def _():
        for h in range(ha):
            mix_scr[:, h * HEAD_DIM:(h + 1) * HEAD_DIM] = oa_ref[0, h]
        for h in range(hb):
            mix_scr[:, (ha + h) * HEAD_DIM:(ha + h + 1) * HEAD_DIM] = ob_ref[0, h]

    acc = jnp.dot(mix_scr[...], w_ref[...], preferred_element_type=F32)
    o_ref[0] = x_ref[0] + mod_ref[0, 2:3, :] * acc


def _outproj(out_a, out_b, w_o, x, mod, tm=512, tn=512):
    b, s, d = x.shape
    ha, hb = out_a.shape[1], out_b.shape[1]
    tm = min(tm, s)
    tn = min(tn, d)
    return pl.pallas_call(
        _outproj_kernel,
        grid=(b, s // tm, d // tn),
        in_specs=[pl.BlockSpec((1, ha, tm, HEAD_DIM), lambda bi, si, j: (bi, 0, si, 0)),
                  pl.BlockSpec((1, hb, tm, HEAD_DIM), lambda bi, si, j: (bi, 0, si, 0)),
                  pl.BlockSpec((d, tn), lambda bi, si, j: (0, j)),
                  pl.BlockSpec((1, tm, tn), lambda bi, si, j: (bi, si, j)),
                  pl.BlockSpec((1, 6, tn), lambda bi, si, j: (bi, 0, j))],
        out_specs=pl.BlockSpec((1, tm, tn), lambda bi, si, j: (bi, si, j)),
        out_shape=jax.ShapeDtypeStruct((b, s, d), F32),
        scratch_shapes=[pltpu.VMEM((tm, d), BF16)],
        compiler_params=_params(("arbitrary", "arbitrary", "arbitrary")),
        name="outproj",
    )(out_a, out_b, w_o, x, mod)


def _top_desc(s, k):
    vals = []
    cur = s
    for _ in range(k):
        mx = jnp.max(cur, axis=0, keepdims=True)
        vals.append(mx)
        cur = jnp.where(cur >= mx, -jnp.inf, cur)
    return vals


def _peer_front_kernel(x_ref, mod_ref, g_ref, w_ref, sk1_ref, sk2_ref,
                       h2_ref, s1_ref, s2_ref, a1_ref, a2_ref, tau_ref, h_scr):
    hh = pl.program_id(2)

    @pl.when(hh == 0)
    def _():
        x = x_ref[0]
        ms = jnp.mean(x * x, axis=-1, keepdims=True)
        y = x * lax.rsqrt(ms + NORM_EPS) * g_ref[...]
        h = (y * (1.0 + mod_ref[0, 4:5, :]) + mod_ref[0, 3:4, :]).astype(BF16)
        h_scr[...] = h
        h2_ref[0] = h

    q = jnp.dot(h_scr[...], w_ref[...], preferred_element_type=F32)
    q1 = q[:, :N_KEYS].astype(BF16)
    q2 = q[:, N_KEYS:].astype(BF16)
    s1 = lax.dot_general(sk1_ref[...], q1, _NT, preferred_element_type=F32)
    s2 = lax.dot_general(sk2_ref[...], q2, _NT, preferred_element_type=F32)
    v1 = _top_desc(s1, PEER_TOPK)
    v2 = _top_desc(s2, PEER_TOPK)
    v1_all = jnp.concatenate(v1, axis=0)
    v2_all = jnp.concatenate(v2, axis=0)
    half = PEER_TOPK // 2
    cands = [v1[0] + v2_all]
    cands += [v1[a] + v2_all[0:half, :] for a in range(1, half)]
    cands += [v1_all[half:, :] + v2[0]]
    cand = jnp.concatenate(cands, axis=0)
    top = _top_desc(cand, PEER_TOPK)
    tau = top[-1]
    z = None
    for t in top:
        e = jnp.exp(t - top[0])
        z = e if z is None else z + e
    s1_ref[0, 0] = s1
    s2_ref[0, 0] = s2
    a1_ref[0, 0] = jnp.exp(s1 - v1[0]) / z
    a2_ref[0, 0] = jnp.exp(s2 - v2[0])
    tau_ref[0, 0] = tau


def _peer_front(x1, mod, norm2_g, w_pq, sk1, sk2, tm=512):
    b, s, d = x1.shape
    tm = min(tm, s)
    qd = 2 * N_KEYS
    tab = jax.ShapeDtypeStruct((b, PEER_HEADS, N_KEYS, s), F32)
    tab_spec = pl.BlockSpec((1, 1, N_KEYS, tm), lambda bi, si, hh: (bi, hh, 0, si))
    return pl.pallas_call(
        _peer_front_kernel,
        grid=(b, s // tm, PEER_HEADS),
        in_specs=[pl.BlockSpec((1, tm, d), lambda bi, si, hh: (bi, si, 0)),
                  pl.BlockSpec((1, 6, d), lambda bi, si, hh: (bi, 0, 0)),
                  pl.BlockSpec((1, d), lambda bi, si, hh: (0, 0)),
                  pl.BlockSpec((d, qd), lambda bi, si, hh: (0, hh)),
                  pl.BlockSpec((N_KEYS, N_KEYS), lambda bi, si, hh: (0, 0)),
                  pl.BlockSpec((N_KEYS, N_KEYS), lambda bi, si, hh: (0, 0))],
        out_specs=[pl.BlockSpec((1, tm, d), lambda bi, si, hh: (bi, si, 0)),
                   tab_spec, tab_spec, tab_spec, tab_spec,
                   pl.BlockSpec((1, 1, 1, tm), lambda bi, si, hh: (bi, hh, 0, si))],
        out_shape=[jax.ShapeDtypeStruct((b, s, d), BF16), tab, tab, tab, tab,
                   jax.ShapeDtypeStruct((b, PEER_HEADS, 1, s), F32)],
        scratch_shapes=[pltpu.VMEM((tm, d), BF16)],
        compiler_params=_params(("arbitrary", "arbitrary", "arbitrary")),
        name="peer_front",
    )(x1, mod, norm2_g, w_pq, sk1, sk2)


def _gelu(a):
    return 0.5 * a * (1.0 + lax.erf(a * (2.0 ** -0.5)))


def _peer_dense_kernel(h2_ref, u_ref, vt_ref, s1_ref, s2_ref, a1_ref, a2_ref, tau_ref,
                       x_ref, mod_ref, o_ref, acc_scr, a_scr, act_scr, *, tt, eb, nblk):
    e = pl.program_id(2)
    rows_per_blk = eb // N_KEYS

    @pl.when(e == 0)
    def _():
        acc_scr[...] = jnp.zeros(acc_scr.shape, F32)
        a_scr[...] = jnp.zeros(a_scr.shape, F32)
        act_scr[...] = jnp.zeros(act_scr.shape, BF16)

    @pl.when(e <= nblk + 1)
    def _():
        acc_scr[...] += jnp.dot(vt_ref[...], act_scr[...], preferred_element_type=F32)

        blk = jnp.clip(e - 1, 0, nblk - 1)
        for il in range(rows_per_blk):
            i = blk * rows_per_blk + il
            rows = slice(il * N_KEYS, (il + 1) * N_KEYS)
            s1_rows = [s1_ref[0, hh, pl.ds(i, 1), :] for hh in range(PEER_HEADS)]
            a1_rows = [a1_ref[0, hh, pl.ds(i, 1), :] for hh in range(PEER_HEADS)]
            for tg in range(tt // 128):
                lanes = slice(tg * 128, (tg + 1) * 128)
                gate = jnp.zeros((N_KEYS, 128), F32)
                for hh in range(PEER_HEADS):
                    c = s1_rows[hh][:, lanes] + s2_ref[0, hh, :, lanes]
                    w = a1_rows[hh][:, lanes] * a2_ref[0, hh, :, lanes]
                    gate = gate + jnp.where(c >= tau_ref[0, hh, :, lanes], w, 0.0)
                act_scr[rows, lanes] = (_gelu(a_scr[rows, lanes]) * gate).astype(BF16)

        a_scr[...] = lax.dot_general(u_ref[...], h2_ref[0], _NT, preferred_element_type=F32)

    @pl.when(e > nblk + 1)
    def _():
        oc = o_ref.shape[2]
        r0 = pl.multiple_of((e - (nblk + 2)) * oc, oc)
        o_ref[0] = x_ref[0] + mod_ref[0, 5:6, :] * acc_scr[pl.ds(r0, oc), :].T


def _peer_dense(h2, u, vt, s1, s2, a1, a2, tau, x1, mod, tt=512, eb=512, oc=1024):
    b, s, d = x1.shape
    nblk = u.shape[0] // eb
    tt = min(tt, s)
    oc = min(oc, d)
    kern = functools.partial(_peer_dense_kernel, tt=tt, eb=eb, nblk=nblk)
    one = pl.Buffered(1)
    tab_spec = pl.BlockSpec((1, PEER_HEADS, N_KEYS, tt), lambda bi, ti, e: (bi, 0, 0, ti),
                            pipeline_mode=one)

    def chunk(e):
        return jnp.maximum(e - (nblk + 2), 0)

    return pl.pallas_call(
        kern,
        grid=(b, s // tt, nblk + 2 + d // oc),
        in_specs=[pl.BlockSpec((1, tt, d), lambda bi, ti, e: (bi, ti, 0), pipeline_mode=one),
                  pl.BlockSpec((eb, d), lambda bi, ti, e: (jnp.minimum(e, nblk - 1), 0)),
                  pl.BlockSpec((d, eb), lambda bi, ti, e: (0, jnp.clip(e - 2, 0, nblk - 1))),
                  tab_spec, tab_spec, tab_spec, tab_spec,
                  pl.BlockSpec((1, PEER_HEADS, 1, tt), lambda bi, ti, e: (bi, 0, 0, ti)),
                  pl.BlockSpec((1, tt, oc), lambda bi, ti, e: (bi, ti, chunk(e))),
                  pl.BlockSpec((1, 6, oc), lambda bi, ti, e: (bi, 0, chunk(e)))],
        out_specs=pl.BlockSpec((1, tt, oc), lambda bi, ti, e: (bi, ti, chunk(e))),
        out_shape=jax.ShapeDtypeStruct((b, s, d), F32),
        scratch_shapes=[pltpu.VMEM((d, tt), F32), pltpu.VMEM((eb, tt), F32),
                        pltpu.VMEM((eb, tt), BF16)],
        compiler_params=_params(("arbitrary", "arbitrary", "arbitrary")),
        name="peer_dense",
    )(h2, u, vt, s1, s2, a1, a2, tau, x1, mod)


def _layer(x, c, w_ada, b_ada, norm1_g, w_in, b_f, q_norm_a, k_norm_a, q_norm_b, k_norm_b,
           rel_bias, w_o, norm2_g, w_pq, sk1, sk2, expert_u, expert_v):
    b, s, d = x.shape
    heads = d // HEAD_DIM
    ha = heads // 2
    hb = heads - ha
    assert ha == hb and s % CHUNK == 0
    width = ha * HEAD_DIM

    c_pad = jnp.zeros((8, d), F32).at[:b].set(c)
    mod = _adaln(c_pad, w_ada, b_ada)[:b].reshape(b, 6, d)

    w_qkv = w_in.astype(BF16)
    w_f = jnp.zeros((d, HEAD_DIM), BF16).at[:, :hb].set(w_qkv[:, 6 * width:])
    b_f_pad = jnp.zeros((1, HEAD_DIM), F32).at[0, :hb].set(b_f)
    scale = HEAD_DIM ** -0.5
    ones = jnp.ones((HEAD_DIM,), F32)
    gains = jnp.stack([q_norm_a * scale, k_norm_a, ones,
                       q_norm_b * (scale * LOG2E), k_norm_b, ones])
    qkv, f_cum, *views = _inproj(x, mod, norm1_g.reshape(1, d), w_qkv, w_f, b_f_pad,
                                 gains.reshape(6, 1, HEAD_DIM))

    bias = _bias_tiles(rel_bias, ha)
    out_a = _dilated([qkv] + views, bias)
    out_b = _fox(qkv, f_cum)
    x1 = _outproj(out_a, out_b, w_o.astype(BF16), x, mod)

    h2, s1, s2, a1, a2, tau = _peer_front(x1, mod, norm2_g.reshape(1, d), w_pq.astype(BF16),
                                          sk1.astype(BF16), sk2.astype(BF16))
    return _peer_dense(h2, expert_u.astype(BF16), expert_v.T.astype(BF16),
                       s1, s2, a1, a2, tau, x1, mod)


def kernel(x, c, w_ada, b_ada, norm1_g, w_in, b_f, q_norm_a, k_norm_a, q_norm_b, k_norm_b, rel_bias,
           w_o, norm2_g, w_pq, sub_keys_1, sub_keys_2, expert_u, expert_v):
    for l in range(w_ada.shape[0]):
        x = _layer(x, c, w_ada[l], b_ada[l], norm1_g[l], w_in[l], b_f[l], q_norm_a[l], k_norm_a[l],
                   q_norm_b[l], k_norm_b[l], rel_bias, w_o[l], norm2_g[l], w_pq[l],
                   sub_keys_1[l], sub_keys_2[l], expert_u[l], expert_v[l])
    return x
```

```python
import functools
import math

import jax
import jax.numpy as jnp
from jax import lax
from jax.experimental import pallas as pl
from jax.experimental.pallas import tpu as pltpu

F32 = jnp.float32
BF16 = jnp.bfloat16

HEAD_DIM = 128
DILATED_BRANCHES = ((128, 1), (512, 4), (2048, 16))
BLOCK_Q = 128
CHUNK = BLOCK_Q * 16
DILATED_GROUP = 16
NUM_BUCKETS = 32
MAX_DISTANCE = 2048
PEER_HEADS = 8
N_KEYS = 128
PEER_TOPK = 16
NORM_EPS = 1e-6
NEG_INF = -1e30
LOG2E = 1.4426950408889634
FOX_BOUND_MAX = 48.0
VMEM_LIMIT = 62 * 1024 * 1024

_NT = (((1,), (1,)), ((), ()))
_TN = (((0,), (0,)), ((), ()))


def _params(sem):
    return pltpu.CompilerParams(dimension_semantics=sem, vmem_limit_bytes=VMEM_LIMIT)


def _adaln_kernel(c_ref, w_ref, b_ref, o_ref):
    c = c_ref[...]
    s = c * jax.nn.sigmoid(c)
    o_ref[...] = jnp.dot(s.astype(BF16), w_ref[...].astype(BF16),
                         preferred_element_type=F32) + b_ref[...]


def _adaln(c_pad, w_ada, b_ada, tn=512):
    rows, d = c_pad.shape
    n = w_ada.shape[1]
    return pl.pallas_call(
        _adaln_kernel,
        grid=(n // tn,),
        in_specs=[pl.BlockSpec((rows, d), lambda j: (0, 0)),
                  pl.BlockSpec((d, tn), lambda j: (0, j)),
                  pl.BlockSpec((1, tn), lambda j: (0, j))],
        out_specs=pl.BlockSpec((rows, tn), lambda j: (0, j)),
        out_shape=jax.ShapeDtypeStruct((rows, n), F32),
        compiler_params=_params(("arbitrary",)),
        name="adaln",
    )(c_pad, w_ada, b_ada.reshape(1, n))


def _t5_bucket(dist):
    max_exact = NUM_BUCKETS // 2
    d32 = jnp.maximum(dist, 1).astype(F32)
    large = max_exact + (jnp.log(d32 / max_exact) / math.log(MAX_DISTANCE / max_exact)
                         * (NUM_BUCKETS - max_exact)).astype(jnp.int32)
    large = jnp.minimum(large, NUM_BUCKETS - 1)
    return jnp.where(dist < max_exact, dist, large)


def _bucket_tiles():
    tiles = []
    for window, dilation in DILATED_BRANCHES:
        nw = window // dilation
        rel = jnp.arange(BLOCK_Q)[:, None] + nw - jnp.arange(BLOCK_Q + nw)[None, :]
        in_win = (rel >= 0) & (rel <= nw)
        bucket = _t5_bucket(jnp.clip(rel, 0, nw) * dilation)
        tiles.append(jnp.where(in_win, bucket, -1).astype(jnp.int32))
    return jnp.stack(tiles)


def _bias_kernel(rb_ref, bucket_ref, o_ref):
    h = pl.program_id(1)
    bucket = bucket_ref[0]
    acc = jnp.full(bucket.shape, NEG_INF, F32)
    for b in range(NUM_BUCKETS):
        acc = jnp.where(bucket == b, rb_ref[b, h], acc)
    o_ref[0, 0] = acc


def _bias_tiles(rel_bias, heads_a):
    buckets = _bucket_tiles()
    nb, bq, bk = buckets.shape
    return pl.pallas_call(
        _bias_kernel,
        grid=(nb, heads_a),
        in_specs=[pl.BlockSpec(memory_space=pltpu.SMEM),
                  pl.BlockSpec((1, bq, bk), lambda d, h: (d, 0, 0))],
        out_specs=pl.BlockSpec((1, 1, bq, bk), lambda d, h: (d, h, 0, 0)),
        out_shape=jax.ShapeDtypeStruct((nb, heads_a, bq, bk), F32),
        compiler_params=_params(("arbitrary", "arbitrary")),
        name="bias_tiles",
    )(rel_bias, buckets)


def _split3(x):
    hi = x.astype(BF16)
    r = x - hi.astype(F32)
    mid = r.astype(BF16)
    lo = (r - mid.astype(F32)).astype(BF16)
    return hi, mid, lo


def _inproj_kernel(x_ref, mod_ref, g_ref, w_ref, wf_ref, bf_ref, gain_ref,
                   qkv_ref, f_ref, *rest, tm, tn, seg_tiles):
    dils = [d for _, d in DILATED_BRANCHES if d > 1]
    view_refs = rest[:len(dils)]
    h_scr, carry_scr, y_scr = rest[len(dils):]
    si = pl.program_id(1)
    j = pl.program_id(2)

    @pl.when(j == 0)
    def _():
        @pl.when(si == 0)
        def _():
            carry_scr[...] = jnp.zeros_like(carry_scr)

        rc = min(tm, 256)
        row = lax.broadcasted_iota(jnp.int32, (rc, rc), 0)
        col = lax.broadcasted_iota(jnp.int32, (rc, rc), 1)
        tri = jnp.where(col <= row, 1.0, 0.0).astype(BF16)

        def chunk(ci, carry):
            r0 = pl.multiple_of(ci * rc, rc)
            x = x_ref[0, pl.ds(r0, rc), :]
            ms = jnp.mean(x * x, axis=-1, keepdims=True)
            y = x * lax.rsqrt(ms + NORM_EPS) * g_ref[...]
            h = y * (1.0 + mod_ref[0, 1:2, :]) + mod_ref[0, 0:1, :]
            hb = h.astype(BF16)
            h_scr[pl.ds(r0, rc), :] = hb
            fz = jnp.dot(hb, wf_ref[...], preferred_element_type=F32) + bf_ref[...]
            lf = jnp.minimum(fz, 0.0) - jnp.log(1.0 + jnp.exp(-jnp.abs(fz)))
            hi, mid, lo = _split3(lf)
            cs = (jnp.dot(tri, hi, preferred_element_type=F32)
                  + jnp.dot(tri, mid, preferred_element_type=F32)
                  + jnp.dot(tri, lo, preferred_element_type=F32)) + carry_scr[0:1, :]
            f_ref[0, pl.ds(r0, rc), :] = cs
            carry_scr[0:1, :] = cs[rc - 1:rc, :]
            return carry

        lax.fori_loop(0, tm // rc, chunk, 0)

    acc = jnp.dot(h_scr[...], w_ref[...], preferred_element_type=F32)
    seg = j // seg_tiles
    is_norm = jnp.logical_and(seg != 2, seg != 5)
    gain = gain_ref[0]
    for hh in range(tn // HEAD_DIM):
        y = acc[:, hh * HEAD_DIM:(hh + 1) * HEAD_DIM]
        ms = jnp.mean(y * y, axis=-1, keepdims=True)
        scale = jnp.where(is_norm, lax.rsqrt(ms + NORM_EPS), 1.0)
        yn = y * scale * gain
        qkv_ref[0, 0, hh] = yn.astype(BF16)
        y_scr[hh * tm:(hh + 1) * tm, :] = yn

    @pl.when(seg < 3)
    def _():
        for vref, d in zip(view_refs, dils):
            for hh in range(tn // HEAD_DIM):
                for r in range(d):
                    vref[0, 0, hh, :, r * HEAD_DIM:(r + 1) * HEAD_DIM] = (
                        y_scr[pl.ds(hh * tm + r, tm // d, stride=d), :].astype(BF16))


def _inproj(x, mod, norm1_g, w_qkv, w_f, b_f, gains, tm=1024, tn=512):
    b, s, d = x.shape
    n = 3 * d
    width = n // 6
    hg = width // HEAD_DIM
    tn = min(tn, width)
    tm = min(tm, s)
    seg_tiles = width // tn
    hpt = tn // HEAD_DIM
    kern = functools.partial(_inproj_kernel, tm=tm, tn=tn, seg_tiles=seg_tiles)
    last_a = 3 * seg_tiles - 1

    def view_map(bi, si, j):
        ja = jnp.minimum(j, last_a)
        return (ja // seg_tiles, bi, ja % seg_tiles, si, 0)

    dils = [d for _, d in DILATED_BRANCHES if d > 1]
    view_specs = [pl.BlockSpec((1, 1, hpt, tm // d, d * HEAD_DIM), view_map) for d in dils]
    view_shapes = [jax.ShapeDtypeStruct((3, b, hg, s // d, d * HEAD_DIM), BF16) for d in dils]
    return pl.pallas_call(
        kern,
        grid=(b, s // tm, n // tn),
        in_specs=[pl.BlockSpec((1, tm, d), lambda bi, si, j: (bi, si, 0),
                               pipeline_mode=pl.Buffered(1)),
                  pl.BlockSpec((1, 6, d), lambda bi, si, j: (bi, 0, 0)),
                  pl.BlockSpec((1, d), lambda bi, si, j: (0, 0)),
                  pl.BlockSpec((d, tn), lambda bi, si, j: (0, j)),
                  pl.BlockSpec((d, HEAD_DIM), lambda bi, si, j: (0, 0)),
                  pl.BlockSpec((1, HEAD_DIM), lambda bi, si, j: (0, 0)),
                  pl.BlockSpec((1, 1, HEAD_DIM), lambda bi, si, j: (j // seg_tiles, 0, 0))],
        out_specs=[pl.BlockSpec((1, 1, hpt, tm, HEAD_DIM),
                                lambda bi, si, j: (j // seg_tiles, bi, j % seg_tiles, si, 0)),
                   pl.BlockSpec((1, tm, HEAD_DIM), lambda bi, si, j: (bi, si, 0))] + view_specs,
        out_shape=[jax.ShapeDtypeStruct((6, b, hg, s, HEAD_DIM), BF16),
                   jax.ShapeDtypeStruct((b, s, HEAD_DIM), F32)] + view_shapes,
        scratch_shapes=[pltpu.VMEM((tm, d), BF16), pltpu.VMEM((8, HEAD_DIM), F32),
                        pltpu.VMEM((hpt * tm, HEAD_DIM), F32)],
        compiler_params=_params(("arbitrary", "arbitrary", "arbitrary")),
        name="inproj",
    )(x, mod, norm1_g, w_qkv, w_f, b_f, gains)


def _dilated_kernel(*refs):
    nb = len(DILATED_BRANCHES)
    q_refs, kc_refs, kp_refs, vc_refs, vp_refs = (refs[i * nb:(i + 1) * nb] for i in range(5))
    bias_ref = refs[5 * nb]
    o_ref = refs[5 * nb + 1]
    scr = refs[5 * nb + 2:]
    kf_scr, vf_scr, num_scr, m_scr, den_scr = (scr[i * nb:(i + 1) * nb] for i in range(5))
    first_chunk = pl.program_id(2) == 0
    bq = BLOCK_Q
    hd = HEAD_DIM
    bnt = (((2,), (2,)), ((0,), (0,)))
    bnn = (((2,), (1,)), ((0,), (0,)))

    for bi, (_, d) in enumerate(DILATED_BRANCHES):
        rows = CHUNK // d
        nblk = rows // bq
        kf_scr[bi][0:bq, :] = kp_refs[bi][0, 0, 0]
        kf_scr[bi][bq:, :] = kc_refs[bi][0, 0, 0]
        vf_scr[bi][0:bq, :] = vp_refs[bi][0, 0, 0]
        vf_scr[bi][bq:, :] = vc_refs[bi][0, 0, 0]
        all_blocks = [(r, n) for r in range(d) for n in range(nblk)]
        ng = DILATED_GROUP
        for g0 in range(0, len(all_blocks), ng):
            blocks = all_blocks[g0:g0 + ng]
            q = jnp.stack([q_refs[bi][0, 0, 0, n * bq:(n + 1) * bq, r * hd:(r + 1) * hd]
                           for r, n in blocks])
            kw = jnp.stack([kf_scr[bi][n * bq:(n + 2) * bq, r * hd:(r + 1) * hd] for r, n in blocks])
            vw = jnp.stack([vf_scr[bi][n * bq:(n + 2) * bq, r * hd:(r + 1) * hd] for r, n in blocks])
            s = lax.dot_general(q, kw, bnt, preferred_element_type=F32)
            gi = g0 + lax.broadcasted_iota(jnp.int32, (ng, 1, 2 * bq), 0)
            col = lax.broadcasted_iota(jnp.int32, (ng, 1, 2 * bq), 2)
            lim = jnp.where(gi % nblk == 0, jnp.where(first_chunk, bq, 0), 0)
            kill = jnp.where(col < lim, NEG_INF, 0.0)
            logits = s + bias_ref[bi, 0] + kill
            m = jnp.max(logits, axis=-1, keepdims=True)
            p = jnp.exp(logits - m)
            den = jnp.sum(p, axis=-1, keepdims=True)
            num = lax.dot_general(p.astype(BF16), vw, bnn, preferred_element_type=F32)
            for g, (r, n) in enumerate(blocks):
                sel = pl.ds(n * bq * d + r, bq, stride=d) if d > 1 else pl.ds(n * bq, bq)
                num_scr[bi][sel, :] = num[g]
                m_scr[bi][sel, :] = jnp.broadcast_to(m[g], (bq, hd))
                den_scr[bi][sel, :] = jnp.broadcast_to(den[g], (bq, hd))

    rc = 512
    for c in range(CHUNK // rc):
        rows = slice(c * rc, (c + 1) * rc)
        ms = [m_scr[bi][rows, :] for bi in range(nb)]
        m_all = functools.reduce(jnp.maximum, ms)
        num_t = None
        den_t = None
        for bi in range(nb):
            w = jnp.exp(ms[bi] - m_all)
            nw = num_scr[bi][rows, :] * w
            dw = den_scr[bi][rows, :] * w
            num_t = nw if num_t is None else num_t + nw
            den_t = dw if den_t is None else den_t + dw
        o_ref[0, 0, rows, :] = (num_t / den_t).astype(o_ref.dtype)


def _dilated(views, bias):
    _, b, ha, s, _ = views[0].shape
    nchunk = s // CHUNK
    q_specs, kc_specs, kp_specs, vc_specs, vp_specs = [], [], [], [], []
    scratch_kv = []
    for (window, d), view in zip(DILATED_BRANCHES, views):
        assert window // d == BLOCK_Q and view.shape[3:] == (s // d, d * HEAD_DIM)
        rows = CHUNK // d
        blk = (1, 1, 1, rows, d * HEAD_DIM)
        pblk = (1, 1, 1, BLOCK_Q, d * HEAD_DIM)
        prev = rows // BLOCK_Q

        def cur_map(seg):
            return lambda bi, h, c: (seg, bi, h, c, 0)

        def prev_map(seg, prev=prev):
            return lambda bi, h, c: (seg, bi, h, jnp.maximum(c * prev - 1, 0), 0)

        q_specs.append(pl.BlockSpec(blk, cur_map(0)))
        kc_specs.append(pl.BlockSpec(blk, cur_map(1)))
        kp_specs.append(pl.BlockSpec(pblk, prev_map(1)))
        vc_specs.append(pl.BlockSpec(blk, cur_map(2)))
        vp_specs.append(pl.BlockSpec(pblk, prev_map(2)))
        scratch_kv.append(pltpu.VMEM((rows + BLOCK_Q, d * HEAD_DIM), BF16))
    nb = len(DILATED_BRANCHES)
    scratch = scratch_kv + scratch_kv + [pltpu.VMEM((CHUNK, HEAD_DIM), F32)] * (3 * nb)
    bias_spec = pl.BlockSpec((nb, 1, BLOCK_Q, 2 * BLOCK_Q), lambda bi, h, c: (0, h, 0, 0))
    return pl.pallas_call(
        _dilated_kernel,
        grid=(b, ha, nchunk),
        in_specs=q_specs + kc_specs + kp_specs + vc_specs + vp_specs + [bias_spec],
        out_specs=pl.BlockSpec((1, 1, CHUNK, HEAD_DIM), lambda bi, h, c: (bi, h, c, 0)),
        out_shape=jax.ShapeDtypeStruct((b, ha, s, HEAD_DIM), BF16),
        scratch_shapes=scratch,
        compiler_params=_params(("arbitrary", "arbitrary", "arbitrary")),
        name="dilated",
    )(*(list(views) * 5), bias)


def _lane_cols(cols, n):
    lane = lax.broadcasted_iota(jnp.int32, (n, HEAD_DIM), 1)
    out = jnp.zeros((n, HEAD_DIM), F32)
    for c, col in enumerate(cols):
        out = jnp.where(lane == c, col, out)
    return out


def _fox_kernel(q_ref, k_ref, v_ref, f_ref, o_ref, ka_scr, va_scr, qa_scr, m_scr, acc_scr, kn_scr,
                *, tq):
    h = pl.program_id(1)
    qi = pl.program_id(2)
    s_len = k_ref.shape[3]
    hd = HEAD_DIM

    def f_col(r0, n):
        ftile = f_ref[0, pl.ds(r0, n), :]
        lane = lax.broadcasted_iota(jnp.int32, ftile.shape, 1)
        return jnp.sum(jnp.where(lane == h, ftile, 0.0), axis=-1, keepdims=True) * LOG2E

    def parts(x):
        return [p.astype(F32) for p in _split3(x)]

    @pl.when(qi == 0)
    def _():
        kn_scr[...] = jnp.zeros(kn_scr.shape, F32)

        def build(ci, carry):
            r0 = pl.multiple_of(ci * tq, tq)
            ext = _lane_cols(parts(-f_col(r0, tq)) + [1.0] * 6, tq)
            k = k_ref[0, 0, 0, pl.ds(r0, tq), :]
            k32 = k.astype(F32)
            kn = jnp.max(jnp.sum(k32 * k32, axis=-1, keepdims=True), axis=0, keepdims=True)
            kn_scr[...] = jnp.maximum(kn_scr[...], jnp.broadcast_to(kn, kn_scr.shape))
            ka_scr[pl.ds(r0, tq), 0:hd] = k
            ka_scr[pl.ds(r0, tq), hd:2 * hd] = ext.astype(BF16)
            va_scr[pl.ds(r0, tq), 0:hd] = v_ref[0, 0, 0, pl.ds(r0, tq), :]
            va_scr[pl.ds(r0, tq), hd:2 * hd] = _lane_cols([1.0], tq).astype(BF16)
            return carry

        lax.fori_loop(0, s_len // tq, build, 0)

    q0 = pl.multiple_of(qi * tq, tq)
    qcols = [1.0] * 3 + parts(f_col(q0, tq))
    qa_scr[:, 0:hd] = q_ref[0, 0, 0]
    qa_scr[:, hd:2 * hd] = _lane_cols(qcols, tq).astype(BF16)

    def scores(k0, width):
        return lax.dot_general(qa_scr[...], ka_scr[pl.ds(k0, width), :], _NT,
                               preferred_element_type=F32)

    def diag_scores():
        row = lax.broadcasted_iota(jnp.int32, (tq, tq), 0)
        col = lax.broadcasted_iota(jnp.int32, (tq, tq), 1)
        return jnp.where(col <= row, scores(q0, tq), NEG_INF)

    def sweep(tile):
        def pair(j, carry):
            tile(pl.multiple_of(j * 2 * tq, 2 * tq), 2 * tq)
            return carry

        lax.fori_loop(0, qi // 2, pair, 0)

        @pl.when(qi % 2 == 1)
        def _():
            tile(pl.multiple_of((qi - 1) * tq, tq), tq)

    def fold_max(s):
        mr = m_scr[...]
        for c in range(s.shape[1] // hd):
            mr = jnp.maximum(mr, s[:, c * hd:(c + 1) * hd])
        m_scr[...] = mr

    q32 = q_ref[0, 0, 0].astype(F32)
    bound = jnp.sqrt(jnp.sum(q32 * q32, axis=-1, keepdims=True) * kn_scr[0:1, 0:1])
    in_range = jnp.max(bound) < FOX_BOUND_MAX

    @pl.when(in_range)
    def _():
        qa_scr[:, hd:2 * hd] = _lane_cols(qcols + parts(-bound), tq).astype(BF16)

    @pl.when(jnp.logical_not(in_range))
    def _():
        m_scr[...] = jnp.full(m_scr.shape, NEG_INF, F32)
        sweep(lambda k0, width: fold_max(scores(k0, width)))
        fold_max(diag_scores())
        m = jnp.max(m_scr[...], axis=-1, keepdims=True)
        qa_scr[:, hd:2 * hd] = _lane_cols(qcols + parts(-m), tq).astype(BF16)

    acc_scr[...] = jnp.zeros(acc_scr.shape, F32)

    def pv_tile(k0, width):
        p = jnp.exp2(scores(k0, width)).astype(BF16)
        acc_scr[...] += jnp.dot(p, va_scr[pl.ds(k0, width), :], preferred_element_type=F32)

    sweep(pv_tile)
    p = jnp.exp2(diag_scores()).astype(BF16)
    acc = acc_scr[...] + jnp.dot(p, va_scr[pl.ds(q0, tq), :], preferred_element_type=F32)
    o_ref[0, 0] = (acc[:, 0:hd] / acc[:, hd:hd + 1]).astype(o_ref.dtype)


def _fox(qkv, f_cum, tq=512):
    _, b, hb, s, _ = qkv.shape
    tq = min(tq, s)
    kern = functools.partial(_fox_kernel, tq=tq)
    return pl.pallas_call(
        kern,
        grid=(b, hb, s // tq),
        in_specs=[pl.BlockSpec((1, 1, 1, tq, HEAD_DIM), lambda bi, h, qi: (3, bi, h, qi, 0)),
                  pl.BlockSpec((1, 1, 1, s, HEAD_DIM), lambda bi, h, qi: (4, bi, h, 0, 0)),
                  pl.BlockSpec((1, 1, 1, s, HEAD_DIM), lambda bi, h, qi: (5, bi, h, 0, 0)),
                  pl.BlockSpec((1, s, HEAD_DIM), lambda bi, h, qi: (bi, 0, 0))],
        out_specs=pl.BlockSpec((1, 1, tq, HEAD_DIM), lambda bi, h, qi: (bi, h, qi, 0)),
        out_shape=jax.ShapeDtypeStruct((b, hb, s, HEAD_DIM), BF16),
        scratch_shapes=[pltpu.VMEM((s, 2 * HEAD_DIM), BF16), pltpu.VMEM((s, 2 * HEAD_DIM), BF16),
                        pltpu.VMEM((tq, 2 * HEAD_DIM), BF16), pltpu.VMEM((tq, HEAD_DIM), F32),
                        pltpu.VMEM((tq, 2 * HEAD_DIM), F32), pltpu.VMEM((8, HEAD_DIM), F32)],
        compiler_params=_params(("arbitrary", "arbitrary", "arbitrary")),
        name="fox",
    )(qkv, qkv, qkv, f_cum)


def _outproj_kernel(oa_ref, ob_ref, w_ref, x_ref, mod_ref, o_ref, mix_scr):
    j = pl.program_id(2)
    ha = oa_ref.shape[1]
    hb = ob_ref.shape[1]

    @pl.when(j == 0)
    def _():
        for h in range(ha):
            mix_scr[:, h * HEAD_DIM:(h + 1) * HEAD_DIM] = oa_ref[0, h]
        for h in range(hb):
            mix_scr[:, (ha + h) * HEAD_DIM:(ha + h + 1) * HEAD_DIM] = ob_ref[0, h]

    acc = jnp.dot(mix_scr[...], w_ref[...], preferred_element_type=F32)
    o_ref[0] = x_ref[0] + mod_ref[0, 2:3, :] * acc


def _outproj(out_a, out_b, w_o, x, mod, tm=512, tn=512):
    b, s, d = x.shape
    ha, hb = out_a.shape[1], out_b.shape[1]
    tm = min(tm, s)
    tn = min(tn, d)
    return pl.pallas_call(
        _outproj_kernel,
        grid=(b, s // tm, d // tn),
        in_specs=[pl.BlockSpec((1, ha, tm, HEAD_DIM), lambda bi, si, j: (bi, 0, si, 0)),
                  pl.BlockSpec((1, hb, tm, HEAD_DIM), lambda bi, si, j: (bi, 0, si, 0)),
                  pl.BlockSpec((d, tn), lambda bi, si, j: (0, j)),
                  pl.BlockSpec((1, tm, tn), lambda bi, si, j: (bi, si, j)),
                  pl.BlockSpec((1, 6, tn), lambda bi, si, j: (bi, 0, j))],
        out_specs=pl.BlockSpec((1, tm, tn), lambda bi, si, j: (bi, si, j)),
        out_shape=jax.ShapeDtypeStruct((b, s, d), F32),
        scratch_shapes=[pltpu.VMEM((tm, d), BF16)],
        compiler_params=_params(("arbitrary", "arbitrary", "arbitrary")),
        name="outproj",
    )(out_a, out_b, w_o, x, mod)


def _sort_network(n):
    pairs = []
    p = 1
    while p < n:
        k = p
        while k >= 1:
            for j in range(k % p, n - k, 2 * k):
                for i in range(min(k, n - j - k)):
                    if (i + j) // (2 * p) == (i + j + k) // (2 * p):
                        pairs.append((i + j, i + j + k))
            k //= 2
        p *= 2
    return pairs


def _top_desc_sorted_lists(s, k, sub=8):
    lists = [s[i * sub:(i + 1) * sub, :] for i in range(k)]
    for i, j in _sort_network(k):
        hi = jnp.maximum(lists[i], lists[j])
        lists[j] = jnp.minimum(lists[i], lists[j])
        lists[i] = hi
    vals = []
    for t in range(k):
        head = lists[0]
        mx = jnp.max(head, axis=0, keepdims=True)
        vals.append(mx)
        taken = head >= mx
        for i in range(k - 1 - t):
            lists[i] = jnp.where(taken, lists[i + 1], lists[i])
    return vals


def _peer_front_kernel(x_ref, mod_ref, g_ref, w_ref, sk1_ref, sk2_ref,
                       h2_ref, s1_ref, s2_ref, a1_ref, a2_ref, tau_ref, h_scr):
    hh = pl.program_id(2)

    @pl.when(hh == 0)
    def _():
        x = x_ref[0]
        ms = jnp.mean(x * x, axis=-1, keepdims=True)
        y = x * lax.rsqrt(ms + NORM_EPS) * g_ref[...]
        h = (y * (1.0 + mod_ref[0, 4:5, :]) + mod_ref[0, 3:4, :]).astype(BF16)
        h_scr[...] = h
        h2_ref[0] = h

    q = jnp.dot(h_scr[...], w_ref[...], preferred_element_type=F32)
    q1 = q[:, :N_KEYS].astype(BF16)
    q2 = q[:, N_KEYS:].astype(BF16)
    s1 = lax.dot_general(sk1_ref[...], q1, _NT, preferred_element_type=F32)
    s2 = lax.dot_general(sk2_ref[...], q2, _NT, preferred_element_type=F32)
    v1 = _top_desc_sorted_lists(s1, PEER_TOPK)
    v2 = _top_desc_sorted_lists(s2, PEER_TOPK)
    v1_all = jnp.concatenate(v1, axis=0)
    v2_all = jnp.concatenate(v2, axis=0)
    half = PEER_TOPK // 2
    cands = [v1[0] + v2_all]
    cands += [v1[a] + v2_all[0:half, :] for a in range(1, half)]
    cands += [v1_all[half:, :] + v2[0]]
    pad_rows = PEER_TOPK * 8 - (PEER_TOPK + 8 * (half - 1) + half)
    cands += [jnp.full((pad_rows, s1.shape[1]), -jnp.inf, F32)]
    cand = jnp.concatenate(cands, axis=0)
    top = _top_desc_sorted_lists(cand, PEER_TOPK)
    tau = top[-1]
    z = None
    for t in top:
        e = jnp.exp(t - top[0])
        z = e if z is None else z + e
    s1_ref[0, 0] = s1
    s2_ref[0, 0] = s2
    a1_ref[0, 0] = jnp.exp(s1 - v1[0]) / z
    a2_ref[0, 0] = jnp.exp(s2 - v2[0])
    tau_ref[0, 0] = tau


def _peer_front(x1, mod, norm2_g, w_pq, sk1, sk2, tm=512):
    b, s, d = x1.shape
    tm = min(tm, s)
    qd = 2 * N_KEYS
    tab = jax.ShapeDtypeStruct((b, PEER_HEADS, N_KEYS, s), F32)
    tab_spec = pl.BlockSpec((1, 1, N_KEYS, tm), lambda bi, si, hh: (bi, hh, 0, si))
    return pl.pallas_call(
        _peer_front_kernel,
        grid=(b, s // tm, PEER_HEADS),
        in_specs=[pl.BlockSpec((1, tm, d), lambda bi, si, hh: (bi, si, 0)),
                  pl.BlockSpec((1, 6, d), lambda bi, si, hh: (bi, 0, 0)),
                  pl.BlockSpec((1, d), lambda bi, si, hh: (0, 0)),
                  pl.BlockSpec((d, qd), lambda bi, si, hh: (0, hh)),
                  pl.BlockSpec((N_KEYS, N_KEYS), lambda bi, si, hh: (0, 0)),
                  pl.BlockSpec((N_KEYS, N_KEYS), lambda bi, si, hh: (0, 0))],
        out_specs=[pl.BlockSpec((1, tm, d), lambda bi, si, hh: (bi, si, 0)),
                   tab_spec, tab_spec, tab_spec, tab_spec,
                   pl.BlockSpec((1, 1, 1, tm), lambda bi, si, hh: (bi, hh, 0, si))],
        out_shape=[jax.ShapeDtypeStruct((b, s, d), BF16), tab, tab, tab, tab,
                   jax.ShapeDtypeStruct((b, PEER_HEADS, 1, s), F32)],
        scratch_shapes=[pltpu.VMEM((tm, d), BF16)],
        compiler_params=_params(("arbitrary", "arbitrary", "arbitrary")),
        name="peer_front",
    )(x1, mod, norm2_g, w_pq, sk1, sk2)


def _gelu(a):
    return 0.5 * a * (1.0 + lax.erf(a * (2.0 ** -0.5)))


def _peer_dense_kernel(h2_ref, u_ref, vt_ref, s1_ref, s2_ref, a1_ref, a2_ref, tau_ref,
                       x_ref, mod_ref, o_ref, acc_scr, a_scr, act_scr, *, tt, eb, nblk):
    e = pl.program_id(2)
    rows_per_blk = eb // N_KEYS

    @pl.when(e == 0)
    def _():
        acc_scr[...] = jnp.zeros(acc_scr.shape, F32)
        a_scr[...] = jnp.zeros(a_scr.shape, F32)

    def stage3():
        acc_scr[...] += jnp.dot(vt_ref[0], act_scr[...], preferred_element_type=F32)

    def stage1():
        a_scr[...] = lax.dot_general(u_ref[...], h2_ref[0], _NT, preferred_element_type=F32)

    def stage2():
        blk = jnp.clip(e - 1, 0, nblk - 1)
        s1_rows = [[s1_ref[0, hh, pl.ds(blk * rows_per_blk + il, 1), :] for hh in range(PEER_HEADS)]
                   for il in range(rows_per_blk)]
        a1_rows = [[a1_ref[0, hh, pl.ds(blk * rows_per_blk + il, 1), :] for hh in range(PEER_HEADS)]
                   for il in range(rows_per_blk)]
        for tg in range(tt // 128):
            lanes = slice(tg * 128, (tg + 1) * 128)
            for il in range(rows_per_blk):
                rows = slice(il * N_KEYS, (il + 1) * N_KEYS)
                gate = jnp.zeros((N_KEYS, 128), F32)
                for hh in range(PEER_HEADS):
                    c = s1_rows[il][hh][:, lanes] + s2_ref[0, hh, :, lanes]
                    w = a1_rows[il][hh][:, lanes] * a2_ref[0, hh, :, lanes]
                    gate = gate + jnp.where(c >= tau_ref[0, hh, :, lanes], w, 0.0)
                act_scr[rows, lanes] = (_gelu(a_scr[rows, lanes]) * gate).astype(BF16)

    @pl.when(e == 0)
    def _():
        act_scr[...] = jnp.zeros(act_scr.shape, BF16)

    @pl.when(e <= nblk + 1)
    def _():
        stage3()
        stage2()
        stage1()

    @pl.when(e > nblk + 1)
    def _():
        oc = o_ref.shape[2]
        r0 = pl.multiple_of((e - (nblk + 2)) * oc, oc)
        o_ref[0] = x_ref[0] + mod_ref[0, 5:6, :] * acc_scr[pl.ds(r0, oc), :].T


PEER_EB = 512


def _peer_dense(h2, u, vt, s1, s2, a1, a2, tau, x1, mod, tt=512, oc=1024):
    b, s, d = x1.shape
    nblk, _, eb = vt.shape
    tt = min(tt, s)
    oc = min(oc, d)
    kern = functools.partial(_peer_dense_kernel, tt=tt, eb=eb, nblk=nblk)
    one = pl.Buffered(1)
    tab_spec = pl.BlockSpec((1, PEER_HEADS, N_KEYS, tt), lambda bi, ti, e: (bi, 0, 0, ti),
                            pipeline_mode=one)

    def chunk(e):
        return jnp.maximum(e - (nblk + 2), 0)

    return pl.pallas_call(
        kern,
        grid=(b, s // tt, nblk + 2 + d // oc),
        in_specs=[pl.BlockSpec((1, tt, d), lambda bi, ti, e: (bi, ti, 0), pipeline_mode=one),
                  pl.BlockSpec((eb, d), lambda bi, ti, e: (jnp.minimum(e, nblk - 1), 0)),
                  pl.BlockSpec((1, d, eb), lambda bi, ti, e: (jnp.clip(e - 2, 0, nblk - 1), 0, 0)),
                  tab_spec, tab_spec, tab_spec, tab_spec,
                  pl.BlockSpec((1, PEER_HEADS, 1, tt), lambda bi, ti, e: (bi, 0, 0, ti)),
                  pl.BlockSpec((1, tt, oc), lambda bi, ti, e: (bi, ti, chunk(e))),
                  pl.BlockSpec((1, 6, oc), lambda bi, ti, e: (bi, 0, chunk(e)))],
        out_specs=pl.BlockSpec((1, tt, oc), lambda bi, ti, e: (bi, ti, chunk(e))),
        out_shape=jax.ShapeDtypeStruct((b, s, d), F32),
        scratch_shapes=[pltpu.VMEM((d, tt), F32), pltpu.VMEM((eb, tt), F32),
                        pltpu.VMEM((eb, tt), BF16)],
        compiler_params=_params(("arbitrary", "arbitrary", "arbitrary")),
        name="peer_dense",
    )(h2, u, vt, s1, s2, a1, a2, tau, x1, mod)


def _layer(x, c, w_ada, b_ada, norm1_g, w_in, b_f, q_norm_a, k_norm_a, q_norm_b, k_norm_b,
           rel_bias, w_o, norm2_g, w_pq, sk1, sk2, expert_u, expert_v):
    b, s, d = x.shape
    heads = d // HEAD_DIM
    ha = heads // 2
    hb = heads - ha
    assert ha == hb and s % CHUNK == 0
    width = ha * HEAD_DIM

    c_pad = jnp.zeros((8, d), F32).at[:b].set(c)
    mod = _adaln(c_pad, w_ada, b_ada)[:b].reshape(b, 6, d)

    w_qkv = w_in.astype(BF16)
    w_f = jnp.zeros((d, HEAD_DIM), BF16).at[:, :hb].set(w_qkv[:, 6 * width:])
    b_f_pad = jnp.zeros((1, HEAD_DIM), F32).at[0, :hb].set(b_f)
    scale = HEAD_DIM ** -0.5
    ones = jnp.ones((HEAD_DIM,), F32)
    gains = jnp.stack([q_norm_a * scale, k_norm_a, ones,
                       q_norm_b * (scale * LOG2E), k_norm_b, ones])
    qkv, f_cum, *views = _inproj(x, mod, norm1_g.reshape(1, d), w_qkv, w_f, b_f_pad,
                                 gains.reshape(6, 1, HEAD_DIM))

    bias = _bias_tiles(rel_bias, ha)
    out_a = _dilated([qkv] + views, bias)
    out_b = _fox(qkv, f_cum)
    x1 = _outproj(out_a, out_b, w_o.astype(BF16), x, mod)

    h2, s1, s2, a1, a2, tau = _peer_front(x1, mod, norm2_g.reshape(1, d), w_pq.astype(BF16),
                                          sk1.astype(BF16), sk2.astype(BF16))
    vt = jnp.transpose(expert_v.reshape(-1, PEER_EB, d), (0, 2, 1)).astype(BF16)
    return _peer_dense(h2, expert_u.astype(BF16), vt, s1, s2, a1, a2, tau, x1, mod)


def kernel(x, c, w_ada, b_ada, norm1_g, w_in, b_f, q_norm_a, k_norm_a, q_norm_b, k_norm_b, rel_bias,
           w_o, norm2_g, w_pq, sub_keys_1, sub_keys_2, expert_u, expert_v):
    for l in range(w_ada.shape[0]):
        x = _layer(x, c, w_ada[l], b_ada[l], norm1_g[l], w_in[l], b_f[l], q_norm_a[l], k_norm_a[l],
                   q_norm_b[l], k_norm_b[l], rel_bias, w_o[l], norm2_g[l], w_pq[l],
                   sub_keys_1[l], sub_keys_2[l], expert_u[l], expert_v[l])
    return x
```

```python
import functools
import math

import jax
import jax.numpy as jnp
from jax import lax
from jax.experimental import pallas as pl
from jax.experimental.pallas import tpu as pltpu

F32 = jnp.float32
BF16 = jnp.bfloat16

HEAD_DIM = 128
DILATED_BRANCHES = ((128, 1), (512, 4), (2048, 16))
BLOCK_Q = 128
CHUNK = BLOCK_Q * 16
DILATED_GROUP = 16
NUM_BUCKETS = 32
MAX_DISTANCE = 2048
PEER_HEADS = 8
N_KEYS = 128
PEER_TOPK = 16
NORM_EPS = 1e-6
NEG_INF = -1e30
LOG2E = 1.4426950408889634
FOX_BOUND_MAX = 48.0
VMEM_LIMIT = 62 * 1024 * 1024

_NT = (((1,), (1,)), ((), ()))
_TN = (((0,), (0,)), ((), ()))


def _params(sem):
    return pltpu.CompilerParams(dimension_semantics=sem, vmem_limit_bytes=VMEM_LIMIT)


def _adaln_kernel(c_ref, w_ref, b_ref, o_ref):
    c = c_ref[...]
    s = c * jax.nn.sigmoid(c)
    o_ref[...] = jnp.dot(s.astype(BF16), w_ref[...].astype(BF16),
                         preferred_element_type=F32) + b_ref[...]


def _adaln(c_pad, w_ada, b_ada, tn=512):
    rows, d = c_pad.shape
    n = w_ada.shape[1]
    return pl.pallas_call(
        _adaln_kernel,
        grid=(n // tn,),
        in_specs=[pl.BlockSpec((rows, d), lambda j: (0, 0)),
                  pl.BlockSpec((d, tn), lambda j: (0, j)),
                  pl.BlockSpec((1, tn), lambda j: (0, j))],
        out_specs=pl.BlockSpec((rows, tn), lambda j: (0, j)),
        out_shape=jax.ShapeDtypeStruct((rows, n), F32),
        compiler_params=_params(("arbitrary",)),
        name="adaln",
    )(c_pad, w_ada, b_ada.reshape(1, n))


def _t5_bucket(dist):
    max_exact = NUM_BUCKETS // 2
    d32 = jnp.maximum(dist, 1).astype(F32)
    large = max_exact + (jnp.log(d32 / max_exact) / math.log(MAX_DISTANCE / max_exact)
                         * (NUM_BUCKETS - max_exact)).astype(jnp.int32)
    large = jnp.minimum(large, NUM_BUCKETS - 1)
    return jnp.where(dist < max_exact, dist, large)


def _bucket_tiles():
    tiles = []
    for window, dilation in DILATED_BRANCHES:
        nw = window // dilation
        rel = jnp.arange(BLOCK_Q)[:, None] + nw - jnp.arange(BLOCK_Q + nw)[None, :]
        in_win = (rel >= 0) & (rel <= nw)
        bucket = _t5_bucket(jnp.clip(rel, 0, nw) * dilation)
        tiles.append(jnp.where(in_win, bucket, -1).astype(jnp.int32))
    return jnp.stack(tiles)


def _bias_kernel(rb_ref, bucket_ref, o_ref):
    h = pl.program_id(1)
    bucket = bucket_ref[0]
    acc = jnp.full(bucket.shape, NEG_INF, F32)
    for b in range(NUM_BUCKETS):
        acc = jnp.where(bucket == b, rb_ref[b, h], acc)
    o_ref[0, 0] = acc


def _bias_tiles(rel_bias, heads_a):
    buckets = _bucket_tiles()
    nb, bq, bk = buckets.shape
    return pl.pallas_call(
        _bias_kernel,
        grid=(nb, heads_a),
        in_specs=[pl.BlockSpec(memory_space=pltpu.SMEM),
                  pl.BlockSpec((1, bq, bk), lambda d, h: (d, 0, 0))],
        out_specs=pl.BlockSpec((1, 1, bq, bk), lambda d, h: (d, h, 0, 0)),
        out_shape=jax.ShapeDtypeStruct((nb, heads_a, bq, bk), F32),
        compiler_params=_params(("arbitrary", "arbitrary")),
        name="bias_tiles",
    )(rel_bias, buckets)


def _split3(x):
    hi = x.astype(BF16)
    r = x - hi.astype(F32)
    mid = r.astype(BF16)
    lo = (r - mid.astype(F32)).astype(BF16)
    return hi, mid, lo


def _inproj_kernel(x_ref, mod_ref, g_ref, w_ref, wf_ref, bf_ref, gain_ref,
                   qkv_ref, f_ref, *rest, tm, tn, seg_tiles):
    dils = [d for _, d in DILATED_BRANCHES if d > 1]
    view_refs = rest[:len(dils)]
    h_scr, carry_scr, y_scr = rest[len(dils):]
    si = pl.program_id(1)
    j = pl.program_id(2)

    @pl.when(j == 0)
    def _():
        @pl.when(si == 0)
        def _():
            carry_scr[...] = jnp.zeros_like(carry_scr)

        rc = min(tm, 256)
        row = lax.broadcasted_iota(jnp.int32, (rc, rc), 0)
        col = lax.broadcasted_iota(jnp.int32, (rc, rc), 1)
        tri = jnp.where(col <= row, 1.0, 0.0).astype(BF16)

        def chunk(ci, carry):
            r0 = pl.multiple_of(ci * rc, rc)
            x = x_ref[0, pl.ds(r0, rc), :]
            ms = jnp.mean(x * x, axis=-1, keepdims=True)
            y = x * lax.rsqrt(ms + NORM_EPS) * g_ref[...]
            h = y * (1.0 + mod_ref[0, 1:2, :]) + mod_ref[0, 0:1, :]
            hb = h.astype(BF16)
            h_scr[pl.ds(r0, rc), :] = hb
            fz = jnp.dot(hb, wf_ref[...], preferred_element_type=F32) + bf_ref[...]
            lf = jnp.minimum(fz, 0.0) - jnp.log(1.0 + jnp.exp(-jnp.abs(fz)))
            hi, mid, lo = _split3(lf)
            cs = (jnp.dot(tri, hi, preferred_element_type=F32)
                  + jnp.dot(tri, mid, preferred_element_type=F32)
                  + jnp.dot(tri, lo, preferred_element_type=F32)) + carry_scr[0:1, :]
            f_ref[0, pl.ds(r0, rc), :] = cs
            carry_scr[0:1, :] = cs[rc - 1:rc, :]
            return carry

        lax.fori_loop(0, tm // rc, chunk, 0)

    acc = jnp.dot(h_scr[...], w_ref[...], preferred_element_type=F32)
    seg = j // seg_tiles
    is_norm = jnp.logical_and(seg != 2, seg != 5)
    gain = gain_ref[0]
    for hh in range(tn // HEAD_DIM):
        y = acc[:, hh * HEAD_DIM:(hh + 1) * HEAD_DIM]
        ms = jnp.mean(y * y, axis=-1, keepdims=True)
        scale = jnp.where(is_norm, lax.rsqrt(ms + NORM_EPS), 1.0)
        yn = y * scale * gain
        qkv_ref[0, 0, hh] = yn.astype(BF16)
        y_scr[hh * tm:(hh + 1) * tm, :] = yn

    @pl.when(seg < 3)
    def _():
        for vref, d in zip(view_refs, dils):
            for hh in range(tn // HEAD_DIM):
                for r in range(d):
                    vref[0, 0, hh, :, r * HEAD_DIM:(r + 1) * HEAD_DIM] = (
                        y_scr[pl.ds(hh * tm + r, tm // d, stride=d), :].astype(BF16))


def _inproj(x, mod, norm1_g, w_qkv, w_f, b_f, gains, tm=1024, tn=512):
    b, s, d = x.shape
    n = 3 * d
    width = n // 6
    hg = width // HEAD_DIM
    tn = min(tn, width)
    tm = min(tm, s)
    seg_tiles = width // tn
    hpt = tn // HEAD_DIM
    kern = functools.partial(_inproj_kernel, tm=tm, tn=tn, seg_tiles=seg_tiles)
    last_a = 3 * seg_tiles - 1

    def view_map(bi, si, j):
        ja = jnp.minimum(j, last_a)
        return (ja // seg_tiles, bi, ja % seg_tiles, si, 0)

    dils = [d for _, d in DILATED_BRANCHES if d > 1]
    view_specs = [pl.BlockSpec((1, 1, hpt, tm // d, d * HEAD_DIM), view_map) for d in dils]
    view_shapes = [jax.ShapeDtypeStruct((3, b, hg, s // d, d * HEAD_DIM), BF16) for d in dils]
    return pl.pallas_call(
        kern,
        grid=(b, s // tm, n // tn),
        in_specs=[pl.BlockSpec((1, tm, d), lambda bi, si, j: (bi, si, 0),
                               pipeline_mode=pl.Buffered(1)),
                  pl.BlockSpec((1, 6, d), lambda bi, si, j: (bi, 0, 0)),
                  pl.BlockSpec((1, d), lambda bi, si, j: (0, 0)),
                  pl.BlockSpec((d, tn), lambda bi, si, j: (0, j)),
                  pl.BlockSpec((d, HEAD_DIM), lambda bi, si, j: (0, 0)),
                  pl.BlockSpec((1, HEAD_DIM), lambda bi, si, j: (0, 0)),
                  pl.BlockSpec((1, 1, HEAD_DIM), lambda bi, si, j: (j // seg_tiles, 0, 0))],
        out_specs=[pl.BlockSpec((1, 1, hpt, tm, HEAD_DIM),
                                lambda bi, si, j: (j // seg_tiles, bi, j % seg_tiles, si, 0)),
                   pl.BlockSpec((1, tm, HEAD_DIM), lambda bi, si, j: (bi, si, 0))] + view_specs,
        out_shape=[jax.ShapeDtypeStruct((6, b, hg, s, HEAD_DIM), BF16),
                   jax.ShapeDtypeStruct((b, s, HEAD_DIM), F32)] + view_shapes,
        scratch_shapes=[pltpu.VMEM((tm, d), BF16), pltpu.VMEM((8, HEAD_DIM), F32),
                        pltpu.VMEM((hpt * tm, HEAD_DIM), F32)],
        compiler_params=_params(("arbitrary", "arbitrary", "arbitrary")),
        name="inproj",
    )(x, mod, norm1_g, w_qkv, w_f, b_f, gains)


def _dilated_kernel(*refs):
    nb = len(DILATED_BRANCHES)
    q_refs, kc_refs, kp_refs, vc_refs, vp_refs = (refs[i * nb:(i + 1) * nb] for i in range(5))
    bias_ref = refs[5 * nb]
    o_ref = refs[5 * nb + 1]
    scr = refs[5 * nb + 2:]
    kf_scr, vf_scr, num_scr, m_scr, den_scr = (scr[i * nb:(i + 1) * nb] for i in range(5))
    first_chunk = pl.program_id(2) == 0
    bq = BLOCK_Q
    hd = HEAD_DIM
    bnt = (((2,), (2,)), ((0,), (0,)))
    bnn = (((2,), (1,)), ((0,), (0,)))

    for bi, (_, d) in enumerate(DILATED_BRANCHES):
        rows = CHUNK // d
        nblk = rows // bq
        kf_scr[bi][0:bq, :] = kp_refs[bi][0, 0, 0]
        kf_scr[bi][bq:, :] = kc_refs[bi][0, 0, 0]
        vf_scr[bi][0:bq, :] = vp_refs[bi][0, 0, 0]
        vf_scr[bi][bq:, :] = vc_refs[bi][0, 0, 0]
        all_blocks = [(r, n) for r in range(d) for n in range(nblk)]
        ng = DILATED_GROUP
        for g0 in range(0, len(all_blocks), ng):
            blocks = all_blocks[g0:g0 + ng]
            q = jnp.stack([q_refs[bi][0, 0, 0, n * bq:(n + 1) * bq, r * hd:(r + 1) * hd]
                           for r, n in blocks])
            kw = jnp.stack([kf_scr[bi][n * bq:(n + 2) * bq, r * hd:(r + 1) * hd] for r, n in blocks])
            vw = jnp.stack([vf_scr[bi][n * bq:(n + 2) * bq, r * hd:(r + 1) * hd] for r, n in blocks])
            s = lax.dot_general(q, kw, bnt, preferred_element_type=F32)
            gi = g0 + lax.broadcasted_iota(jnp.int32, (ng, 1, 2 * bq), 0)
            col = lax.broadcasted_iota(jnp.int32, (ng, 1, 2 * bq), 2)
            lim = jnp.where(gi % nblk == 0, jnp.where(first_chunk, bq, 0), 0)
            kill = jnp.where(col < lim, NEG_INF, 0.0)
            logits = s + bias_ref[bi, 0] + kill
            m = jnp.max(logits, axis=-1, keepdims=True)
            p = jnp.exp(logits - m)
            den = jnp.sum(p, axis=-1, keepdims=True)
            num = lax.dot_general(p.astype(BF16), vw, bnn, preferred_element_type=F32)
            for g, (r, n) in enumerate(blocks):
                sel = pl.ds(n * bq * d + r, bq, stride=d) if d > 1 else pl.ds(n * bq, bq)
                num_scr[bi][sel, :] = num[g]
                m_scr[bi][sel, :] = jnp.broadcast_to(m[g], (bq, hd))
                den_scr[bi][sel, :] = jnp.broadcast_to(den[g], (bq, hd))

    rc = 512
    for c in range(CHUNK // rc):
        rows = slice(c * rc, (c + 1) * rc)
        ms = [m_scr[bi][rows, :] for bi in range(nb)]
        m_all = functools.reduce(jnp.maximum, ms)
        num_t = None
        den_t = None
        for bi in range(nb):
            w = jnp.exp(ms[bi] - m_all)
            nw = num_scr[bi][rows, :] * w
            dw = den_scr[bi][rows, :] * w
            num_t = nw if num_t is None else num_t + nw
            den_t = dw if den_t is None else den_t + dw
        o_ref[0, 0, rows, :] = (num_t / den_t).astype(o_ref.dtype)


def _dilated(views, bias):
    _, b, ha, s, _ = views[0].shape
    nchunk = s // CHUNK
    q_specs, kc_specs, kp_specs, vc_specs, vp_specs = [], [], [], [], []
    scratch_kv = []
    for (window, d), view in zip(DILATED_BRANCHES, views):
        assert window // d == BLOCK_Q and view.shape[3:] == (s // d, d * HEAD_DIM)
        rows = CHUNK // d
        blk = (1, 1, 1, rows, d * HEAD_DIM)
        pblk = (1, 1, 1, BLOCK_Q, d * HEAD_DIM)
        prev = rows // BLOCK_Q

        def cur_map(seg):
            return lambda bi, h, c: (seg, bi, h, c, 0)

        def prev_map(seg, prev=prev):
            return lambda bi, h, c: (seg, bi, h, jnp.maximum(c * prev - 1, 0), 0)

        q_specs.append(pl.BlockSpec(blk, cur_map(0)))
        kc_specs.append(pl.BlockSpec(blk, cur_map(1)))
        kp_specs.append(pl.BlockSpec(pblk, prev_map(1)))
        vc_specs.append(pl.BlockSpec(blk, cur_map(2)))
        vp_specs.append(pl.BlockSpec(pblk, prev_map(2)))
        scratch_kv.append(pltpu.VMEM((rows + BLOCK_Q, d * HEAD_DIM), BF16))
    nb = len(DILATED_BRANCHES)
    scratch = scratch_kv + scratch_kv + [pltpu.VMEM((CHUNK, HEAD_DIM), F32)] * (3 * nb)
    bias_spec = pl.BlockSpec((nb, 1, BLOCK_Q, 2 * BLOCK_Q), lambda bi, h, c: (0, h, 0, 0))
    return pl.pallas_call(
        _dilated_kernel,
        grid=(b, ha, nchunk),
        in_specs=q_specs + kc_specs + kp_specs + vc_specs + vp_specs + [bias_spec],
        out_specs=pl.BlockSpec((1, 1, CHUNK, HEAD_DIM), lambda bi, h, c: (bi, h, c, 0)),
        out_shape=jax.ShapeDtypeStruct((b, ha, s, HEAD_DIM), BF16),
        scratch_shapes=scratch,
        compiler_params=_params(("arbitrary", "arbitrary", "arbitrary")),
        name="dilated",
    )(*(list(views) * 5), bias)


def _lane_cols(cols, n):
    lane = lax.broadcasted_iota(jnp.int32, (n, HEAD_DIM), 1)
    out = jnp.zeros((n, HEAD_DIM), F32)
    for c, col in enumerate(cols):
        out = jnp.where(lane == c, col, out)
    return out


def _fox_kernel(q_ref, k_ref, v_ref, f_ref, o_ref, ka_scr, va_scr, qa_scr, m_scr, acc_scr, kn_scr,
                *, tq):
    h = pl.program_id(1)
    qi = pl.program_id(2)
    s_len = k_ref.shape[3]
    hd = HEAD_DIM

    def f_col(r0, n):
        ftile = f_ref[0, pl.ds(r0, n), :]
        lane = lax.broadcasted_iota(jnp.int32, ftile.shape, 1)
        return jnp.sum(jnp.where(lane == h, ftile, 0.0), axis=-1, keepdims=True) * LOG2E

    def parts(x):
        return [p.astype(F32) for p in _split3(x)]

    @pl.when(qi == 0)
    def _():
        kn_scr[...] = jnp.zeros(kn_scr.shape, F32)

        def build(ci, carry):
            r0 = pl.multiple_of(ci * tq, tq)
            ext = _lane_cols(parts(-f_col(r0, tq)) + [1.0] * 6, tq)
            k = k_ref[0, 0, 0, pl.ds(r0, tq), :]
            k32 = k.astype(F32)
            kn = jnp.max(jnp.sum(k32 * k32, axis=-1, keepdims=True), axis=0, keepdims=True)
            kn_scr[...] = jnp.maximum(kn_scr[...], jnp.broadcast_to(kn, kn_scr.shape))
            ka_scr[pl.ds(r0, tq), 0:hd] = k
            ka_scr[pl.ds(r0, tq), hd:2 * hd] = ext.astype(BF16)
            va_scr[pl.ds(r0, tq), 0:hd] = v_ref[0, 0, 0, pl.ds(r0, tq), :]
            va_scr[pl.ds(r0, tq), hd:2 * hd] = _lane_cols([1.0], tq).astype(BF16)
            return carry

        lax.fori_loop(0, s_len // tq, build, 0)

    q0 = pl.multiple_of(qi * tq, tq)
    qcols = [1.0] * 3 + parts(f_col(q0, tq))
    qa_scr[:, 0:hd] = q_ref[0, 0, 0]
    qa_scr[:, hd:2 * hd] = _lane_cols(qcols, tq).astype(BF16)

    def scores(k0, width):
        return lax.dot_general(qa_scr[...], ka_scr[pl.ds(k0, width), :], _NT,
                               preferred_element_type=F32)

    def diag_scores():
        row = lax.broadcasted_iota(jnp.int32, (tq, tq), 0)
        col = lax.broadcasted_iota(jnp.int32, (tq, tq), 1)
        return jnp.where(col <= row, scores(q0, tq), NEG_INF)

    def sweep(tile):
        def pair(j, carry):
            tile(pl.multiple_of(j * 2 * tq, 2 * tq), 2 * tq)
            return carry

        lax.fori_loop(0, qi // 2, pair, 0)

        @pl.when(qi % 2 == 1)
        def _():
            tile(pl.multiple_of((qi - 1) * tq, tq), tq)

    def fold_max(s):
        mr = m_scr[...]
        for c in range(s.shape[1] // hd):
            mr = jnp.maximum(mr, s[:, c * hd:(c + 1) * hd])
        m_scr[...] = mr

    q32 = q_ref[0, 0, 0].astype(F32)
    bound = jnp.sqrt(jnp.sum(q32 * q32, axis=-1, keepdims=True) * kn_scr[0:1, 0:1])
    in_range = jnp.max(bound) < FOX_BOUND_MAX

    @pl.when(in_range)
    def _():
        qa_scr[:, hd:2 * hd] = _lane_cols(qcols + parts(-bound), tq).astype(BF16)

    @pl.when(jnp.logical_not(in_range))
    def _():
        m_scr[...] = jnp.full(m_scr.shape, NEG_INF, F32)
        sweep(lambda k0, width: fold_max(scores(k0, width)))
        fold_max(diag_scores())
        m = jnp.max(m_scr[...], axis=-1, keepdims=True)
        qa_scr[:, hd:2 * hd] = _lane_cols(qcols + parts(-m), tq).astype(BF16)

    acc_scr[...] = jnp.zeros(acc_scr.shape, F32)

    def pv_tile(k0, width):
        p = jnp.exp2(scores(k0, width)).astype(BF16)
        acc_scr[...] += jnp.dot(p, va_scr[pl.ds(k0, width), :], preferred_element_type=F32)

    sweep(pv_tile)
    p = jnp.exp2(diag_scores()).astype(BF16)
    acc = acc_scr[...] + jnp.dot(p, va_scr[pl.ds(q0, tq), :], preferred_element_type=F32)
    o_ref[0, 0] = (acc[:, 0:hd] / acc[:, hd:hd + 1]).astype(o_ref.dtype)


def _fox(qkv, f_cum, tq=512):
    _, b, hb, s, _ = qkv.shape
    tq = min(tq, s)
    kern = functools.partial(_fox_kernel, tq=tq)
    return pl.pallas_call(
        kern,
        grid=(b, hb, s // tq),
        in_specs=[pl.BlockSpec((1, 1, 1, tq, HEAD_DIM), lambda bi, h, qi: (3, bi, h, qi, 0)),
                  pl.BlockSpec((1, 1, 1, s, HEAD_DIM), lambda bi, h, qi: (4, bi, h, 0, 0)),
                  pl.BlockSpec((1, 1, 1, s, HEAD_DIM), lambda bi, h, qi: (5, bi, h, 0, 0)),
                  pl.BlockSpec((1, s, HEAD_DIM), lambda bi, h, qi: (bi, 0, 0))],
        out_specs=pl.BlockSpec((1, 1, tq, HEAD_DIM), lambda bi, h, qi: (bi, h, qi, 0)),
        out_shape=jax.ShapeDtypeStruct((b, hb, s, HEAD_DIM), BF16),
        scratch_shapes=[pltpu.VMEM((s, 2 * HEAD_DIM), BF16), pltpu.VMEM((s, 2 * HEAD_DIM), BF16),
                        pltpu.VMEM((tq, 2 * HEAD_DIM), BF16), pltpu.VMEM((tq, HEAD_DIM), F32),
                        pltpu.VMEM((tq, 2 * HEAD_DIM), F32), pltpu.VMEM((8, HEAD_DIM), F32)],
        compiler_params=_params(("arbitrary", "arbitrary", "arbitrary")),
        name="fox",
    )(qkv, qkv, qkv, f_cum)


def _outproj_kernel(oa_ref, ob_ref, w_ref, x_ref, mod_ref, o_ref, mix_scr):
    j = pl.program_id(2)
    ha = oa_ref.shape[1]
    hb = ob_ref.shape[1]

    @pl.when(j == 0)
    def _():
        for h in range(ha):
            mix_scr[:, h * HEAD_DIM:(h + 1) * HEAD_DIM] = oa_ref[0, h]
        for h in range(hb):
            mix_scr[:, (ha + h) * HEAD_DIM:(ha + h + 1) * HEAD_DIM] = ob_ref[0, h]

    acc = jnp.dot(mix_scr[...], w_ref[...], preferred_element_type=F32)
    o_ref[0] = x_ref[0] + mod_ref[0, 2:3, :] * acc


def _outproj(out_a, out_b, w_o, x, mod, tm=512, tn=512):
    b, s, d = x.shape
    ha, hb = out_a.shape[1], out_b.shape[1]
    tm = min(tm, s)
    tn = min(tn, d)
    return pl.pallas_call(
        _outproj_kernel,
        grid=(b, s // tm, d // tn),
        in_specs=[pl.BlockSpec((1, ha, tm, HEAD_DIM), lambda bi, si, j: (bi, 0, si, 0)),
                  pl.BlockSpec((1, hb, tm, HEAD_DIM), lambda bi, si, j: (bi, 0, si, 0)),
                  pl.BlockSpec((d, tn), lambda bi, si, j: (0, j)),
                  pl.BlockSpec((1, tm, tn), lambda bi, si, j: (bi, si, j)),
                  pl.BlockSpec((1, 6, tn), lambda bi, si, j: (bi, 0, j))],
        out_specs=pl.BlockSpec((1, tm, tn), lambda bi, si, j: (bi, si, j)),
        out_shape=jax.ShapeDtypeStruct((b, s, d), F32),
        scratch_shapes=[pltpu.VMEM((tm, d), BF16)],
        compiler_params=_params(("arbitrary", "arbitrary", "arbitrary")),
        name="outproj",
    )(out_a, out_b, w_o, x, mod)


def _sort_network(n):
    pairs = []
    p = 1
    while p < n:
        k = p
        while k >= 1:
            for j in range(k % p, n - k, 2 * k):
                for i in range(min(k, n - j - k)):
                    if (i + j) // (2 * p) == (i + j + k) // (2 * p):
                        pairs.append((i + j, i + j + k))
            k //= 2
        p *= 2
    return pairs


def _top_desc_sorted_lists(s, k, sub=8):
    lists = [s[i * sub:(i + 1) * sub, :] for i in range(k)]
    for i, j in _sort_network(k):
        hi = jnp.maximum(lists[i], lists[j])
        lists[j] = jnp.minimum(lists[i], lists[j])
        lists[i] = hi
    vals = []
    for t in range(k):
        head = lists[0]
        mx = jnp.max(head, axis=0, keepdims=True)
        vals.append(mx)
        taken = head >= mx
        for i in range(k - 1 - t):
            lists[i] = jnp.where(taken, lists[i + 1], lists[i])
    return vals


def _peer_front_kernel(x_ref, mod_ref, g_ref, w_ref, sk1_ref, sk2_ref,
                       h2_ref, a1_ref, a2_ref, tau_ref, h_scr):
    hh = pl.program_id(2)

    @pl.when(hh == 0)
    def _():
        x = x_ref[0]
        ms = jnp.mean(x * x, axis=-1, keepdims=True)
        y = x * lax.rsqrt(ms + NORM_EPS) * g_ref[...]
        h = (y * (1.0 + mod_ref[0, 4:5, :]) + mod_ref[0, 3:4, :]).astype(BF16)
        h_scr[...] = h
        h2_ref[0] = h

    q = jnp.dot(h_scr[...], w_ref[...], preferred_element_type=F32)
    q1 = q[:, :N_KEYS].astype(BF16)
    q2 = q[:, N_KEYS:].astype(BF16)
    s1 = lax.dot_general(sk1_ref[...], q1, _NT, preferred_element_type=F32)
    s2 = lax.dot_general(sk2_ref[...], q2, _NT, preferred_element_type=F32)
    v1 = _top_desc_sorted_lists(s1, PEER_TOPK)
    v2 = _top_desc_sorted_lists(s2, PEER_TOPK)
    v1_all = jnp.concatenate(v1, axis=0)
    v2_all = jnp.concatenate(v2, axis=0)
    half = PEER_TOPK // 2
    cands = [v1[0] + v2_all]
    cands += [v1[a] + v2_all[0:half, :] for a in range(1, half)]
    cands += [v1_all[half:, :] + v2[0]]
    real = jnp.concatenate(cands, axis=0)
    pad_rows = PEER_TOPK * 8 - real.shape[0]
    cand = jnp.concatenate([real, jnp.full((pad_rows, s1.shape[1]), -jnp.inf, F32)], axis=0)
    top = _top_desc_sorted_lists(cand, PEER_TOPK)
    tau = top[-1]
    z = None
    for t in top:
        e = jnp.exp(t - top[0])
        z = e if z is None else z + e

    def f1(s):
        return 0.5 * jnp.exp(s - v1[0]) / z

    def f2(s):
        return jnp.exp(s - v2[0])

    g1 = f1(v1_all)
    g2 = f2(v2_all)
    prods = [g1[0:1, :] * g2]
    prods += [g1[a:a + 1, :] * g2[0:half, :] for a in range(1, half)]
    prods += [g1[half:, :] * g2[0:1, :]]
    prod = jnp.concatenate(prods, axis=0)
    a1_ref[0, 0] = f1(s1)
    a2_ref[0, 0] = f2(s2)
    tau_ref[0, 0] = jnp.min(jnp.where(real >= tau, prod, jnp.inf), axis=0, keepdims=True)


def _peer_front(x1, mod, norm2_g, w_pq, sk1, sk2, tm=512):
    b, s, d = x1.shape
    tm = min(tm, s)
    qd = 2 * N_KEYS
    tab = jax.ShapeDtypeStruct((b, PEER_HEADS, N_KEYS, s), F32)
    tab_spec = pl.BlockSpec((1, 1, N_KEYS, tm), lambda bi, si, hh: (bi, hh, 0, si))
    return pl.pallas_call(
        _peer_front_kernel,
        grid=(b, s // tm, PEER_HEADS),
        in_specs=[pl.BlockSpec((1, tm, d), lambda bi, si, hh: (bi, si, 0)),
                  pl.BlockSpec((1, 6, d), lambda bi, si, hh: (bi, 0, 0)),
                  pl.BlockSpec((1, d), lambda bi, si, hh: (0, 0)),
                  pl.BlockSpec((d, qd), lambda bi, si, hh: (0, hh)),
                  pl.BlockSpec((N_KEYS, N_KEYS), lambda bi, si, hh: (0, 0)),
                  pl.BlockSpec((N_KEYS, N_KEYS), lambda bi, si, hh: (0, 0))],
        out_specs=[pl.BlockSpec((1, tm, d), lambda bi, si, hh: (bi, si, 0)),
                   tab_spec, tab_spec,
                   pl.BlockSpec((1, 1, 1, tm), lambda bi, si, hh: (bi, hh, 0, si))],
        out_shape=[jax.ShapeDtypeStruct((b, s, d), BF16), tab, tab,
                   jax.ShapeDtypeStruct((b, PEER_HEADS, 1, s), F32)],
        scratch_shapes=[pltpu.VMEM((tm, d), BF16)],
        compiler_params=_params(("arbitrary", "arbitrary", "arbitrary")),
        name="peer_front",
    )(x1, mod, norm2_g, w_pq, sk1, sk2)


def _gelu_x2(a):
    return a * (1.0 + lax.erf(a * (2.0 ** -0.5)))


def _peer_dense_kernel(h2_ref, u_ref, vt_ref, a1_ref, a2_ref, tau_ref,
                       x_ref, mod_ref, o_ref, acc_scr, a_scr, act_scr, *, tt, eb, nblk):
    e = pl.program_id(2)
    rows_per_blk = eb // N_KEYS

    @pl.when(e == 0)
    def _():
        acc_scr[...] = jnp.zeros(acc_scr.shape, F32)
        a_scr[...] = jnp.zeros(a_scr.shape, F32)

    def stage3():
        acc_scr[...] += jnp.dot(vt_ref[0], act_scr[...], preferred_element_type=F32)

    def stage1():
        a_scr[...] = lax.dot_general(u_ref[...], h2_ref[0], _NT, preferred_element_type=F32)

    def stage2():
        blk = jnp.clip(e - 1, 0, nblk - 1)
        a1_rows = [[a1_ref[0, hh, pl.ds(blk * rows_per_blk + il, 1), :] for hh in range(PEER_HEADS)]
                   for il in range(rows_per_blk)]
        for tg in range(tt // 128):
            lanes = slice(tg * 128, (tg + 1) * 128)
            for il in range(rows_per_blk):
                rows = slice(il * N_KEYS, (il + 1) * N_KEYS)
                gate = jnp.zeros((N_KEYS, 128), F32)
                for hh in range(PEER_HEADS):
                    w = a1_rows[il][hh][:, lanes] * a2_ref[0, hh, :, lanes]
                    gate = gate + jnp.where(w >= tau_ref[0, hh, :, lanes], w, 0.0)
                act_scr[rows, lanes] = (_gelu_x2(a_scr[rows, lanes]) * gate).astype(BF16)

    @pl.when(e == 0)
    def _():
        act_scr[...] = jnp.zeros(act_scr.shape, BF16)

    @pl.when(e <= nblk + 1)
    def _():
        stage3()
        stage2()
        stage1()

    @pl.when(e > nblk + 1)
    def _():
        oc = o_ref.shape[2]
        r0 = pl.multiple_of((e - (nblk + 2)) * oc, oc)
        o_ref[0] = x_ref[0] + mod_ref[0, 5:6, :] * acc_scr[pl.ds(r0, oc), :].T


PEER_EB = 512


def _peer_dense(h2, u, vt, a1, a2, tau, x1, mod, tt=512, oc=1024):
    b, s, d = x1.shape
    nblk, _, eb = vt.shape
    tt = min(tt, s)
    oc = min(oc, d)
    kern = functools.partial(_peer_dense_kernel, tt=tt, eb=eb, nblk=nblk)
    one = pl.Buffered(1)
    tab_spec = pl.BlockSpec((1, PEER_HEADS, N_KEYS, tt), lambda bi, ti, e: (bi, 0, 0, ti))

    def chunk(e):
        return jnp.maximum(e - (nblk + 2), 0)

    return pl.pallas_call(
        kern,
        grid=(b, s // tt, nblk + 2 + d // oc),
        in_specs=[pl.BlockSpec((1, tt, d), lambda bi, ti, e: (bi, ti, 0), pipeline_mode=one),
                  pl.BlockSpec((eb, d), lambda bi, ti, e: (jnp.minimum(e, nblk - 1), 0)),
                  pl.BlockSpec((1, d, eb), lambda bi, ti, e: (jnp.clip(e - 2, 0, nblk - 1), 0, 0)),
                  tab_spec, tab_spec,
                  pl.BlockSpec((1, PEER_HEADS, 1, tt), lambda bi, ti, e: (bi, 0, 0, ti)),
                  pl.BlockSpec((1, tt, oc), lambda bi, ti, e: (bi, ti, chunk(e))),
                  pl.BlockSpec((1, 6, oc), lambda bi, ti, e: (bi, 0, chunk(e)))],
        out_specs=pl.BlockSpec((1, tt, oc), lambda bi, ti, e: (bi, ti, chunk(e))),
        out_shape=jax.ShapeDtypeStruct((b, s, d), F32),
        scratch_shapes=[pltpu.VMEM((d, tt), F32), pltpu.VMEM((eb, tt), F32),
                        pltpu.VMEM((eb, tt), BF16)],
        compiler_params=_params(("arbitrary", "arbitrary", "arbitrary")),
        name="peer_dense",
    )(h2, u, vt, a1, a2, tau, x1, mod)


def _layer(x, c, w_ada, b_ada, norm1_g, w_in, b_f, q_norm_a, k_norm_a, q_norm_b, k_norm_b,
           rel_bias, w_o, norm2_g, w_pq, sk1, sk2, expert_u, expert_v):
    b, s, d = x.shape
    heads = d // HEAD_DIM
    ha = heads // 2
    hb = heads - ha
    assert ha == hb and s % CHUNK == 0
    width = ha * HEAD_DIM

    c_pad = jnp.zeros((8, d), F32).at[:b].set(c)
    mod = _adaln(c_pad, w_ada, b_ada)[:b].reshape(b, 6, d)

    w_qkv = w_in.astype(BF16)
    w_f = jnp.zeros((d, HEAD_DIM), BF16).at[:, :hb].set(w_qkv[:, 6 * width:])
    b_f_pad = jnp.zeros((1, HEAD_DIM), F32).at[0, :hb].set(b_f)
    scale = HEAD_DIM ** -0.5
    ones = jnp.ones((HEAD_DIM,), F32)
    gains = jnp.stack([q_norm_a * scale, k_norm_a, ones,
                       q_norm_b * (scale * LOG2E), k_norm_b, ones])
    qkv, f_cum, *views = _inproj(x, mod, norm1_g.reshape(1, d), w_qkv, w_f, b_f_pad,
                                 gains.reshape(6, 1, HEAD_DIM))

    bias = _bias_tiles(rel_bias, ha)
    out_a = _dilated([qkv] + views, bias)
    out_b = _fox(qkv, f_cum)
    x1 = _outproj(out_a, out_b, w_o.astype(BF16), x, mod)

    h2, a1, a2, tau = _peer_front(x1, mod, norm2_g.reshape(1, d), w_pq.astype(BF16),
                                  sk1.astype(BF16), sk2.astype(BF16))
    vt = jnp.transpose(expert_v.reshape(-1, PEER_EB, d), (0, 2, 1)).astype(BF16)
    return _peer_dense(h2, expert_u.astype(BF16), vt, a1, a2, tau, x1, mod)


def kernel(x, c, w_ada, b_ada, norm1_g, w_in, b_f, q_norm_a, k_norm_a, q_norm_b, k_norm_b, rel_bias,
           w_o, norm2_g, w_pq, sub_keys_1, sub_keys_2, expert_u, expert_v):
    for l in range(w_ada.shape[0]):
        x = _layer(x, c, w_ada[l], b_ada[l], norm1_g[l], w_in[l], b_f[l], q_norm_a[l], k_norm_a[l],
                   q_norm_b[l], k_norm_b[l], rel_bias, w_o[l], norm2_g[l], w_pq[l],
                   sub_keys_1[l], sub_keys_2[l], expert_u[l], expert_v[l])
    return x
```

```python
import functools
import math

import jax
import jax.numpy as jnp
from jax import lax
from jax.experimental import pallas as pl
from jax.experimental.pallas import tpu as pltpu

F32 = jnp.float32
BF16 = jnp.bfloat16

HEAD_DIM = 128
DILATED_BRANCHES = ((128, 1), (512, 4), (2048, 16))
BLOCK_Q = 128
CHUNK = BLOCK_Q * 16
DILATED_GROUP = 16
NUM_BUCKETS = 32
MAX_DISTANCE = 2048
PEER_HEADS = 8
N_KEYS = 128
PEER_TOPK = 16
NORM_EPS = 1e-6
NEG_INF = -1e30
LOG2E = 1.4426950408889634
FOX_BOUND_MAX = 48.0
VMEM_LIMIT = 62 * 1024 * 1024

_NT = (((1,), (1,)), ((), ()))


def _params(sem):
    return pltpu.CompilerParams(dimension_semantics=sem, vmem_limit_bytes=VMEM_LIMIT)


def _adaln_kernel(c_ref, w_ref, b_ref, o_ref):
    c = c_ref[...]
    s = c * jax.nn.sigmoid(c)
    o_ref[...] = jnp.dot(s.astype(BF16), w_ref[...].astype(BF16),
                         preferred_element_type=F32) + b_ref[...]


def _adaln(c_pad, w_ada, b_ada, tn=512):
    rows, d = c_pad.shape
    n = w_ada.shape[1]
    return pl.pallas_call(
        _adaln_kernel,
        grid=(n // tn,),
        in_specs=[pl.BlockSpec((rows, d), lambda j: (0, 0)),
                  pl.BlockSpec((d, tn), lambda j: (0, j)),
                  pl.BlockSpec((1, tn), lambda j: (0, j))],
        out_specs=pl.BlockSpec((rows, tn), lambda j: (0, j)),
        out_shape=jax.ShapeDtypeStruct((rows, n), F32),
        compiler_params=_params(("arbitrary",)),
        name="adaln",
    )(c_pad, w_ada, b_ada.reshape(1, n))


def _t5_bucket(dist):
    max_exact = NUM_BUCKETS // 2
    d32 = jnp.maximum(dist, 1).astype(F32)
    large = max_exact + (jnp.log(d32 / max_exact) / math.log(MAX_DISTANCE / max_exact)
                         * (NUM_BUCKETS - max_exact)).astype(jnp.int32)
    large = jnp.minimum(large, NUM_BUCKETS - 1)
    return jnp.where(dist < max_exact, dist, large)


def _bucket_tiles():
    tiles = []
    for window, dilation in DILATED_BRANCHES:
        nw = window // dilation
        rel = jnp.arange(BLOCK_Q)[:, None] + nw - jnp.arange(BLOCK_Q + nw)[None, :]
        in_win = (rel >= 0) & (rel <= nw)
        bucket = _t5_bucket(jnp.clip(rel, 0, nw) * dilation)
        tiles.append(jnp.where(in_win, bucket, -1).astype(jnp.int32))
    return jnp.stack(tiles)


def _bias_kernel(rb_ref, bucket_ref, o_ref):
    h = pl.program_id(1)
    bucket = bucket_ref[0]
    acc = jnp.full(bucket.shape, NEG_INF, F32)
    for b in range(NUM_BUCKETS):
        acc = jnp.where(bucket == b, rb_ref[b, h], acc)
    o_ref[0, 0] = acc


def _bias_tiles(rel_bias, heads_a):
    buckets = _bucket_tiles()
    nb, bq, bk = buckets.shape
    return pl.pallas_call(
        _bias_kernel,
        grid=(nb, heads_a),
        in_specs=[pl.BlockSpec(memory_space=pltpu.SMEM),
                  pl.BlockSpec((1, bq, bk), lambda d, h: (d, 0, 0))],
        out_specs=pl.BlockSpec((1, 1, bq, bk), lambda d, h: (d, h, 0, 0)),
        out_shape=jax.ShapeDtypeStruct((nb, heads_a, bq, bk), F32),
        compiler_params=_params(("arbitrary", "arbitrary")),
        name="bias_tiles",
    )(rel_bias, buckets)


def _split3(x):
    hi = x.astype(BF16)
    r = x - hi.astype(F32)
    mid = r.astype(BF16)
    lo = (r - mid.astype(F32)).astype(BF16)
    return hi, mid, lo


def _inproj_kernel(x_ref, mod_ref, g_ref, w_ref, wf_ref, bf_ref, gain_ref,
                   qkv_ref, f_ref, *rest, tm, tn, seg_tiles):
    dils = [d for _, d in DILATED_BRANCHES if d > 1]
    view_refs = rest[:len(dils)]
    h_scr, carry_scr, y_scr = rest[len(dils):]
    si = pl.program_id(1)
    j = pl.program_id(2)

    @pl.when(j == 0)
    def _():
        @pl.when(si == 0)
        def _():
            carry_scr[...] = jnp.zeros_like(carry_scr)

        rc = min(tm, 256)
        row = lax.broadcasted_iota(jnp.int32, (rc, rc), 0)
        col = lax.broadcasted_iota(jnp.int32, (rc, rc), 1)
        tri = jnp.where(col <= row, 1.0, 0.0).astype(BF16)

        def chunk(ci, carry):
            r0 = pl.multiple_of(ci * rc, rc)
            x = x_ref[0, pl.ds(r0, rc), :]
            ms = jnp.mean(x * x, axis=-1, keepdims=True)
            y = x * lax.rsqrt(ms + NORM_EPS) * g_ref[...]
            h = y * (1.0 + mod_ref[0, 1:2, :]) + mod_ref[0, 0:1, :]
            hb = h.astype(BF16)
            h_scr[pl.ds(r0, rc), :] = hb
            fz = jnp.dot(hb, wf_ref[...], preferred_element_type=F32) + bf_ref[...]
            lf = jnp.minimum(fz, 0.0) - jnp.log(1.0 + jnp.exp(-jnp.abs(fz)))
            hi, mid, lo = _split3(lf)
            cs = (jnp.dot(tri, hi, preferred_element_type=F32)
                  + jnp.dot(tri, mid, preferred_element_type=F32)
                  + jnp.dot(tri, lo, preferred_element_type=F32)) + carry_scr[0:1, :]
            f_ref[0, pl.ds(r0, rc), :] = cs
            carry_scr[0:1, :] = cs[rc - 1:rc, :]
            return carry

        lax.fori_loop(0, tm // rc, chunk, 0)

    acc = jnp.dot(h_scr[...], w_ref[...], preferred_element_type=F32)
    seg = j // seg_tiles
    is_norm = jnp.logical_and(seg != 2, seg != 5)
    gain = gain_ref[0]
    for hh in range(tn // HEAD_DIM):
        y = acc[:, hh * HEAD_DIM:(hh + 1) * HEAD_DIM]
        ms = jnp.mean(y * y, axis=-1, keepdims=True)
        scale = jnp.where(is_norm, lax.rsqrt(ms + NORM_EPS), 1.0)
        yn = y * scale * gain
        qkv_ref[0, 0, hh] = yn.astype(BF16)
        y_scr[hh * tm:(hh + 1) * tm, :] = yn

    @pl.when(seg < 3)
    def _():
        for vref, d in zip(view_refs, dils):
            for hh in range(tn // HEAD_DIM):
                for r in range(d):
                    vref[0, 0, hh, :, r * HEAD_DIM:(r + 1) * HEAD_DIM] = (
                        y_scr[pl.ds(hh * tm + r, tm // d, stride=d), :].astype(BF16))


def _inproj(x, mod, norm1_g, w_qkv, w_f, b_f, gains, tm=1024, tn=512):
    b, s, d = x.shape
    n = 3 * d
    width = n // 6
    hg = width // HEAD_DIM
    tn = min(tn, width)
    tm = min(tm, s)
    seg_tiles = width // tn
    hpt = tn // HEAD_DIM
    kern = functools.partial(_inproj_kernel, tm=tm, tn=tn, seg_tiles=seg_tiles)
    last_a = 3 * seg_tiles - 1

    def view_map(bi, si, j):
        ja = jnp.minimum(j, last_a)
        return (ja // seg_tiles, bi, ja % seg_tiles, si, 0)

    dils = [d for _, d in DILATED_BRANCHES if d > 1]
    view_specs = [pl.BlockSpec((1, 1, hpt, tm // d, d * HEAD_DIM), view_map) for d in dils]
    view_shapes = [jax.ShapeDtypeStruct((3, b, hg, s // d, d * HEAD_DIM), BF16) for d in dils]
    return pl.pallas_call(
        kern,
        grid=(b, s // tm, n // tn),
        in_specs=[pl.BlockSpec((1, tm, d), lambda bi, si, j: (bi, si, 0),
                               pipeline_mode=pl.Buffered(1)),
                  pl.BlockSpec((1, 6, d), lambda bi, si, j: (bi, 0, 0)),
                  pl.BlockSpec((1, d), lambda bi, si, j: (0, 0)),
                  pl.BlockSpec((d, tn), lambda bi, si, j: (0, j)),
                  pl.BlockSpec((d, HEAD_DIM), lambda bi, si, j: (0, 0)),
                  pl.BlockSpec((1, HEAD_DIM), lambda bi, si, j: (0, 0)),
                  pl.BlockSpec((1, 1, HEAD_DIM), lambda bi, si, j: (j // seg_tiles, 0, 0))],
        out_specs=[pl.BlockSpec((1, 1, hpt, tm, HEAD_DIM),
                                lambda bi, si, j: (j // seg_tiles, bi, j % seg_tiles, si, 0)),
                   pl.BlockSpec((1, tm, HEAD_DIM), lambda bi, si, j: (bi, si, 0))] + view_specs,
        out_shape=[jax.ShapeDtypeStruct((6, b, hg, s, HEAD_DIM), BF16),
                   jax.ShapeDtypeStruct((b, s, HEAD_DIM), F32)] + view_shapes,
        scratch_shapes=[pltpu.VMEM((tm, d), BF16), pltpu.VMEM((8, HEAD_DIM), F32),
                        pltpu.VMEM((hpt * tm, HEAD_DIM), F32)],
        compiler_params=_params(("arbitrary", "arbitrary", "arbitrary")),
        name="inproj",
    )(x, mod, norm1_g, w_qkv, w_f, b_f, gains)


def _dilated_kernel(*refs):
    nb = len(DILATED_BRANCHES)
    q_refs, kc_refs, kp_refs, vc_refs, vp_refs = (refs[i * nb:(i + 1) * nb] for i in range(5))
    bias_ref = refs[5 * nb]
    o_ref = refs[5 * nb + 1]
    scr = refs[5 * nb + 2:]
    kf_scr, vf_scr, num_scr, m_scr, den_scr = (scr[i * nb:(i + 1) * nb] for i in range(5))
    first_chunk = pl.program_id(2) == 0
    bq = BLOCK_Q
    hd = HEAD_DIM
    bnt = (((2,), (2,)), ((0,), (0,)))
    bnn = (((2,), (1,)), ((0,), (0,)))

    for bi, (_, d) in enumerate(DILATED_BRANCHES):
        rows = CHUNK // d
        nblk = rows // bq
        kf_scr[bi][0:bq, :] = kp_refs[bi][0, 0, 0]
        kf_scr[bi][bq:, :] = kc_refs[bi][0, 0, 0]
        vf_scr[bi][0:bq, :] = vp_refs[bi][0, 0, 0]
        vf_scr[bi][bq:, :] = vc_refs[bi][0, 0, 0]
        all_blocks = [(r, n) for r in range(d) for n in range(nblk)]
        ng = DILATED_GROUP
        for g0 in range(0, len(all_blocks), ng):
            blocks = all_blocks[g0:g0 + ng]
            q = jnp.stack([q_refs[bi][0, 0, 0, n * bq:(n + 1) * bq, r * hd:(r + 1) * hd]
                           for r, n in blocks])
            kw = jnp.stack([kf_scr[bi][n * bq:(n + 2) * bq, r * hd:(r + 1) * hd] for r, n in blocks])
            vw = jnp.stack([vf_scr[bi][n * bq:(n + 2) * bq, r * hd:(r + 1) * hd] for r, n in blocks])
            s = lax.dot_general(q, kw, bnt, preferred_element_type=F32)
            gi = g0 + lax.broadcasted_iota(jnp.int32, (ng, 1, 2 * bq), 0)
            col = lax.broadcasted_iota(jnp.int32, (ng, 1, 2 * bq), 2)
            lim = jnp.where(gi % nblk == 0, jnp.where(first_chunk, bq, 0), 0)
            kill = jnp.where(col < lim, NEG_INF, 0.0)
            logits = s + bias_ref[bi, 0] + kill
            m = jnp.max(logits, axis=-1, keepdims=True)
            p = jnp.exp(logits - m)
            den = jnp.sum(p, axis=-1, keepdims=True)
            num = lax.dot_general(p.astype(BF16), vw, bnn, preferred_element_type=F32)
            for g, (r, n) in enumerate(blocks):
                sel = pl.ds(n * bq * d + r, bq, stride=d) if d > 1 else pl.ds(n * bq, bq)
                num_scr[bi][sel, :] = num[g]
                m_scr[bi][sel, :] = jnp.broadcast_to(m[g], (bq, hd))
                den_scr[bi][sel, :] = jnp.broadcast_to(den[g], (bq, hd))

    rc = 512
    for c in range(CHUNK // rc):
        rows = slice(c * rc, (c + 1) * rc)
        ms = [m_scr[bi][rows, :] for bi in range(nb)]
        m_all = functools.reduce(jnp.maximum, ms)
        num_t = None
        den_t = None
        for bi in range(nb):
            w = jnp.exp(ms[bi] - m_all)
            nw = num_scr[bi][rows, :] * w
            dw = den_scr[bi][rows, :] * w
            num_t = nw if num_t is None else num_t + nw
            den_t = dw if den_t is None else den_t + dw
        o_ref[0, 0, rows, :] = (num_t / den_t).astype(o_ref.dtype)


def _dilated(views, bias):
    _, b, ha, s, _ = views[0].shape
    nchunk = s // CHUNK
    q_specs, kc_specs, kp_specs, vc_specs, vp_specs = [], [], [], [], []
    scratch_kv = []
    for (window, d), view in zip(DILATED_BRANCHES, views):
        assert window // d == BLOCK_Q and view.shape[3:] == (s // d, d * HEAD_DIM)
        rows = CHUNK // d
        blk = (1, 1, 1, rows, d * HEAD_DIM)
        pblk = (1, 1, 1, BLOCK_Q, d * HEAD_DIM)
        prev = rows // BLOCK_Q

        def cur_map(seg):
            return lambda bi, h, c: (seg, bi, h, c, 0)

        def prev_map(seg, prev=prev):
            return lambda bi, h, c: (seg, bi, h, jnp.maximum(c * prev - 1, 0), 0)

        q_specs.append(pl.BlockSpec(blk, cur_map(0)))
        kc_specs.append(pl.BlockSpec(blk, cur_map(1)))
        kp_specs.append(pl.BlockSpec(pblk, prev_map(1)))
        vc_specs.append(pl.BlockSpec(blk, cur_map(2)))
        vp_specs.append(pl.BlockSpec(pblk, prev_map(2)))
        scratch_kv.append(pltpu.VMEM((rows + BLOCK_Q, d * HEAD_DIM), BF16))
    nb = len(DILATED_BRANCHES)
    scratch = scratch_kv + scratch_kv + [pltpu.VMEM((CHUNK, HEAD_DIM), F32)] * (3 * nb)
    bias_spec = pl.BlockSpec((nb, 1, BLOCK_Q, 2 * BLOCK_Q), lambda bi, h, c: (0, h, 0, 0))
    return pl.pallas_call(
        _dilated_kernel,
        grid=(b, ha, nchunk),
        in_specs=q_specs + kc_specs + kp_specs + vc_specs + vp_specs + [bias_spec],
        out_specs=pl.BlockSpec((1, 1, CHUNK, HEAD_DIM), lambda bi, h, c: (bi, h, c, 0)),
        out_shape=jax.ShapeDtypeStruct((b, ha, s, HEAD_DIM), BF16),
        scratch_shapes=scratch,
        compiler_params=_params(("arbitrary", "arbitrary", "arbitrary")),
        name="dilated",
    )(*(list(views) * 5), bias)


def _lane_cols(cols, n):
    lane = lax.broadcasted_iota(jnp.int32, (n, HEAD_DIM), 1)
    out = jnp.zeros((n, HEAD_DIM), F32)
    for c, col in enumerate(cols):
        out = jnp.where(lane == c, col, out)
    return out


def _fox_kernel(q_ref, k_ref, v_ref, f_ref, o_ref, ka_scr, va_scr, qa_scr, m_scr, acc_scr, kn_scr,
                *, tq):
    h = pl.program_id(1)
    qi = pl.program_id(2)
    s_len = k_ref.shape[3]
    hd = HEAD_DIM

    def f_col(r0, n):
        ftile = f_ref[0, pl.ds(r0, n), :]
        lane = lax.broadcasted_iota(jnp.int32, ftile.shape, 1)
        return jnp.sum(jnp.where(lane == h, ftile, 0.0), axis=-1, keepdims=True) * LOG2E

    def parts(x):
        return [p.astype(F32) for p in _split3(x)]

    @pl.when(qi == 0)
    def _():
        kn_scr[...] = jnp.zeros(kn_scr.shape, F32)

        def build(ci, carry):
            r0 = pl.multiple_of(ci * tq, tq)
            ext = _lane_cols(parts(-f_col(r0, tq)) + [1.0] * 6, tq)
            k = k_ref[0, 0, 0, pl.ds(r0, tq), :]
            k32 = k.astype(F32)
            kn = jnp.max(jnp.sum(k32 * k32, axis=-1, keepdims=True), axis=0, keepdims=True)
            kn_scr[...] = jnp.maximum(kn_scr[...], jnp.broadcast_to(kn, kn_scr.shape))
            ka_scr[pl.ds(r0, tq), 0:hd] = k
            ka_scr[pl.ds(r0, tq), hd:2 * hd] = ext.astype(BF16)
            va_scr[pl.ds(r0, tq), 0:hd] = v_ref[0, 0, 0, pl.ds(r0, tq), :]
            va_scr[pl.ds(r0, tq), hd:2 * hd] = _lane_cols([1.0], tq).astype(BF16)
            return carry

        lax.fori_loop(0, s_len // tq, build, 0)

    q0 = pl.multiple_of(qi * tq, tq)
    qcols = [1.0] * 3 + parts(f_col(q0, tq))
    qa_scr[:, 0:hd] = q_ref[0, 0, 0]
    qa_scr[:, hd:2 * hd] = _lane_cols(qcols, tq).astype(BF16)

    def scores(k0, width):
        return lax.dot_general(qa_scr[...], ka_scr[pl.ds(k0, width), :], _NT,
                               preferred_element_type=F32)

    def causal_scores(k0, width):
        row = lax.broadcasted_iota(jnp.int32, (tq, width), 0)
        col = lax.broadcasted_iota(jnp.int32, (tq, width), 1)
        return jnp.where(col <= row + (width - tq), scores(k0, width), NEG_INF)

    def sweep(tile):
        def pair(j, carry):
            k0 = pl.multiple_of(j * 2 * tq, 2 * tq)
            tile(scores(k0, 2 * tq), k0, 2 * tq)
            return carry

        lax.fori_loop(0, qi // 2, pair, 0)

        @pl.when(qi % 2 == 1)
        def _():
            k0 = pl.multiple_of((qi - 1) * tq, tq)
            tile(causal_scores(k0, 2 * tq), k0, 2 * tq)

        @pl.when(qi % 2 == 0)
        def _():
            tile(causal_scores(q0, tq), q0, tq)

    def fold_max(s):
        mr = m_scr[...]
        for c in range(s.shape[1] // hd):
            mr = jnp.maximum(mr, s[:, c * hd:(c + 1) * hd])
        m_scr[...] = mr

    q32 = q_ref[0, 0, 0].astype(F32)
    bound = jnp.sqrt(jnp.sum(q32 * q32, axis=-1, keepdims=True) * kn_scr[0:1, 0:1])
    in_range = jnp.max(bound) < FOX_BOUND_MAX

    @pl.when(in_range)
    def _():
        qa_scr[:, hd:2 * hd] = _lane_cols(qcols + parts(-bound), tq).astype(BF16)

    @pl.when(jnp.logical_not(in_range))
    def _():
        m_scr[...] = jnp.full(m_scr.shape, NEG_INF, F32)
        sweep(lambda s, k0, width: fold_max(s))
        m = jnp.max(m_scr[...], axis=-1, keepdims=True)
        qa_scr[:, hd:2 * hd] = _lane_cols(qcols + parts(-m), tq).astype(BF16)

    acc_scr[...] = jnp.zeros(acc_scr.shape, F32)

    def pv_tile(s, k0, width):
        p = jnp.exp2(s).astype(BF16)
        acc_scr[...] += jnp.dot(p, va_scr[pl.ds(k0, width), :], preferred_element_type=F32)

    sweep(pv_tile)
    acc = acc_scr[...]
    o_ref[0, 0] = (acc[:, 0:hd] / acc[:, hd:hd + 1]).astype(o_ref.dtype)


def _fox(qkv, f_cum, tq=512):
    _, b, hb, s, _ = qkv.shape
    tq = min(tq, s)
    kern = functools.partial(_fox_kernel, tq=tq)
    return pl.pallas_call(
        kern,
        grid=(b, hb, s // tq),
        in_specs=[pl.BlockSpec((1, 1, 1, tq, HEAD_DIM), lambda bi, h, qi: (3, bi, h, qi, 0)),
                  pl.BlockSpec((1, 1, 1, s, HEAD_DIM), lambda bi, h, qi: (4, bi, h, 0, 0)),
                  pl.BlockSpec((1, 1, 1, s, HEAD_DIM), lambda bi, h, qi: (5, bi, h, 0, 0)),
                  pl.BlockSpec((1, s, HEAD_DIM), lambda bi, h, qi: (bi, 0, 0))],
        out_specs=pl.BlockSpec((1, 1, tq, HEAD_DIM), lambda bi, h, qi: (bi, h, qi, 0)),
        out_shape=jax.ShapeDtypeStruct((b, hb, s, HEAD_DIM), BF16),
        scratch_shapes=[pltpu.VMEM((s, 2 * HEAD_DIM), BF16), pltpu.VMEM((s, 2 * HEAD_DIM), BF16),
                        pltpu.VMEM((tq, 2 * HEAD_DIM), BF16), pltpu.VMEM((tq, HEAD_DIM), F32),
                        pltpu.VMEM((tq, 2 * HEAD_DIM), F32), pltpu.VMEM((8, HEAD_DIM), F32)],
        compiler_params=_params(("arbitrary", "arbitrary", "arbitrary")),
        name="fox",
    )(qkv, qkv, qkv, f_cum)


def _outproj_kernel(oa_ref, ob_ref, w_ref, x_ref, mod_ref, o_ref, mix_scr):
    j = pl.program_id(2)
    ha = oa_ref.shape[1]
    hb = ob_ref.shape[1]

    @pl.when(j == 0)
    def _():
        for h in range(ha):
            mix_scr[:, h * HEAD_DIM:(h + 1) * HEAD_DIM] = oa_ref[0, h]
        for h in range(hb):
            mix_scr[:, (ha + h) * HEAD_DIM:(ha + h + 1) * HEAD_DIM] = ob_ref[0, h]

    acc = jnp.dot(mix_scr[...], w_ref[...], preferred_element_type=F32)
    o_ref[0] = x_ref[0] + mod_ref[0, 2:3, :] * acc


def _outproj(out_a, out_b, w_o, x, mod, tm=512, tn=512):
    b, s, d = x.shape
    ha, hb = out_a.shape[1], out_b.shape[1]
    tm = min(tm, s)
    tn = min(tn, d)
    return pl.pallas_call(
        _outproj_kernel,
        grid=(b, s // tm, d // tn),
        in_specs=[pl.BlockSpec((1, ha, tm, HEAD_DIM), lambda bi, si, j: (bi, 0, si, 0)),
                  pl.BlockSpec((1, hb, tm, HEAD_DIM), lambda bi, si, j: (bi, 0, si, 0)),
                  pl.BlockSpec((d, tn), lambda bi, si, j: (0, j)),
                  pl.BlockSpec((1, tm, tn), lambda bi, si, j: (bi, si, j)),
                  pl.BlockSpec((1, 6, tn), lambda bi, si, j: (bi, 0, j))],
        out_specs=pl.BlockSpec((1, tm, tn), lambda bi, si, j: (bi, si, j)),
        out_shape=jax.ShapeDtypeStruct((b, s, d), F32),
        scratch_shapes=[pltpu.VMEM((tm, d), BF16)],
        compiler_params=_params(("arbitrary", "arbitrary", "arbitrary")),
        name="outproj",
    )(out_a, out_b, w_o, x, mod)


def _sort_network(n):
    pairs = []
    p = 1
    while p < n:
        k = p
        while k >= 1:
            for j in range(k % p, n - k, 2 * k):
                for i in range(min(k, n - j - k)):
                    if (i + j) // (2 * p) == (i + j + k) // (2 * p):
                        pairs.append((i + j, i + j + k))
            k //= 2
        p *= 2
    return pairs


def _top_desc_sorted_lists(s, k, sub=8):
    lists = [s[i * sub:(i + 1) * sub, :] for i in range(k)]
    for i, j in _sort_network(k):
        hi = jnp.maximum(lists[i], lists[j])
        lists[j] = jnp.minimum(lists[i], lists[j])
        lists[i] = hi
    vals = []
    for t in range(k):
        head = lists[0]
        mx = jnp.max(head, axis=0, keepdims=True)
        vals.append(mx)
        taken = head >= mx
        for i in range(k - 1 - t):
            lists[i] = jnp.where(taken, lists[i + 1], lists[i])
    return vals


def _peer_front_kernel(x_ref, mod_ref, g_ref, w_ref, sk1_ref, sk2_ref,
                       h2_ref, a1_ref, a2_ref, tau_ref, h_scr):
    hh = pl.program_id(2)

    @pl.when(hh == 0)
    def _():
        x = x_ref[0]
        ms = jnp.mean(x * x, axis=-1, keepdims=True)
        y = x * lax.rsqrt(ms + NORM_EPS) * g_ref[...]
        h = (y * (1.0 + mod_ref[0, 4:5, :]) + mod_ref[0, 3:4, :]).astype(BF16)
        h_scr[...] = h
        h2_ref[0] = h

    q = jnp.dot(h_scr[...], w_ref[...], preferred_element_type=F32)
    q1 = q[:, :N_KEYS].astype(BF16)
    q2 = q[:, N_KEYS:].astype(BF16)
    s1 = lax.dot_general(sk1_ref[...], q1, _NT, preferred_element_type=F32)
    s2 = lax.dot_general(sk2_ref[...], q2, _NT, preferred_element_type=F32)
    v1 = _top_desc_sorted_lists(s1, PEER_TOPK)
    v2 = _top_desc_sorted_lists(s2, PEER_TOPK)
    v1_all = jnp.concatenate(v1, axis=0)
    v2_all = jnp.concatenate(v2, axis=0)
    half = PEER_TOPK // 2
    cands = [v1[0] + v2_all]
    cands += [v1[a] + v2_all[0:half, :] for a in range(1, half)]
    cands += [v1_all[half:, :] + v2[0]]
    real = jnp.concatenate(cands, axis=0)
    pad_rows = PEER_TOPK * 8 - real.shape[0]
    cand = jnp.concatenate([real, jnp.full((pad_rows, s1.shape[1]), -jnp.inf, F32)], axis=0)
    top = _top_desc_sorted_lists(cand, PEER_TOPK)
    tau = top[-1]
    z = None
    for t in top:
        e = jnp.exp(t - top[0])
        z = e if z is None else z + e

    def f1(s):
        return 0.5 * jnp.exp(s - v1[0]) / z

    def f2(s):
        return jnp.exp(s - v2[0])

    g1 = f1(v1_all)
    g2 = f2(v2_all)
    prods = [g1[0:1, :] * g2]
    prods += [g1[a:a + 1, :] * g2[0:half, :] for a in range(1, half)]
    prods += [g1[half:, :] * g2[0:1, :]]
    prod = jnp.concatenate(prods, axis=0)
    a1_ref[0, 0] = f1(s1)
    a2_ref[0, 0] = f2(s2)
    tau_ref[0, 0] = jnp.min(jnp.where(real >= tau, prod, jnp.inf), axis=0, keepdims=True)


def _peer_front(x1, mod, norm2_g, w_pq, sk1, sk2, tm=512):
    b, s, d = x1.shape
    tm = min(tm, s)
    qd = 2 * N_KEYS
    tab = jax.ShapeDtypeStruct((b, PEER_HEADS, N_KEYS, s), F32)
    tab_spec = pl.BlockSpec((1, 1, N_KEYS, tm), lambda bi, si, hh: (bi, hh, 0, si))
    return pl.pallas_call(
        _peer_front_kernel,
        grid=(b, s // tm, PEER_HEADS),
        in_specs=[pl.BlockSpec((1, tm, d), lambda bi, si, hh: (bi, si, 0)),
                  pl.BlockSpec((1, 6, d), lambda bi, si, hh: (bi, 0, 0)),
                  pl.BlockSpec((1, d), lambda bi, si, hh: (0, 0)),
                  pl.BlockSpec((d, qd), lambda bi, si, hh: (0, hh)),
                  pl.BlockSpec((N_KEYS, N_KEYS), lambda bi, si, hh: (0, 0)),
                  pl.BlockSpec((N_KEYS, N_KEYS), lambda bi, si, hh: (0, 0))],
        out_specs=[pl.BlockSpec((1, tm, d), lambda bi, si, hh: (bi, si, 0)),
                   tab_spec, tab_spec,
                   pl.BlockSpec((1, 1, 1, tm), lambda bi, si, hh: (bi, hh, 0, si))],
        out_shape=[jax.ShapeDtypeStruct((b, s, d), BF16), tab, tab,
                   jax.ShapeDtypeStruct((b, PEER_HEADS, 1, s), F32)],
        scratch_shapes=[pltpu.VMEM((tm, d), BF16)],
        compiler_params=_params(("arbitrary", "arbitrary", "arbitrary")),
        name="peer_front",
    )(x1, mod, norm2_g, w_pq, sk1, sk2)


def _gelu_x2(a):
    return a * (1.0 + lax.erf(a * (2.0 ** -0.5)))


def _peer_dense_kernel(h2_ref, u_ref, vt_ref, a1_ref, a2_ref, tau_ref,
                       x_ref, mod_ref, o_ref, acc_scr, a_scr, act_scr, *, tt, eb, nblk):
    e = pl.program_id(2)
    rows_per_blk = eb // N_KEYS

    @pl.when(e == 0)
    def _():
        acc_scr[...] = jnp.zeros(acc_scr.shape, F32)
        a_scr[...] = jnp.zeros(a_scr.shape, F32)

    def stage3():
        acc_scr[...] += jnp.dot(vt_ref[0], act_scr[...], preferred_element_type=F32)

    def stage1():
        a_scr[...] = lax.dot_general(u_ref[...], h2_ref[0], _NT, preferred_element_type=F32)

    def stage2():
        blk = jnp.clip(e - 1, 0, nblk - 1)
        a1_rows = [[a1_ref[0, hh, pl.ds(blk * rows_per_blk + il, 1), :] for hh in range(PEER_HEADS)]
                   for il in range(rows_per_blk)]
        for tg in range(tt // 128):
            lanes = slice(tg * 128, (tg + 1) * 128)
            for il in range(rows_per_blk):
                rows = slice(il * N_KEYS, (il + 1) * N_KEYS)
                gate = jnp.zeros((N_KEYS, 128), F32)
                for hh in range(PEER_HEADS):
                    w = a1_rows[il][hh][:, lanes] * a2_ref[0, hh, :, lanes]
                    gate = gate + jnp.where(w >= tau_ref[0, hh, :, lanes], w, 0.0)
                act_scr[rows, lanes] = (_gelu_x2(a_scr[rows, lanes]) * gate).astype(BF16)

    @pl.when(e == 0)
    def _():
        act_scr[...] = jnp.zeros(act_scr.shape, BF16)

    @pl.when(e <= nblk + 1)
    def _():
        stage3()
        stage2()
        stage1()

    @pl.when(e > nblk + 1)
    def _():
        oc = o_ref.shape[2]
        r0 = pl.multiple_of((e - (nblk + 2)) * oc, oc)
        o_ref[0] = x_ref[0] + mod_ref[0, 5:6, :] * acc_scr[pl.ds(r0, oc), :].T


PEER_EB = 512


def _peer_dense(h2, u, vt, a1, a2, tau, x1, mod, tt=512, oc=1024):
    b, s, d = x1.shape
    nblk, _, eb = vt.shape
    tt = min(tt, s)
    oc = min(oc, d)
    kern = functools.partial(_peer_dense_kernel, tt=tt, eb=eb, nblk=nblk)
    tab_spec = pl.BlockSpec((1, PEER_HEADS, N_KEYS, tt), lambda bi, ti, e: (bi, 0, 0, ti))

    def chunk(e):
        return jnp.maximum(e - (nblk + 2), 0)

    return pl.pallas_call(
        kern,
        grid=(b, s // tt, nblk + 2 + d // oc),
        in_specs=[pl.BlockSpec((1, tt, d), lambda bi, ti, e: (bi, ti, 0)),
                  pl.BlockSpec((eb, d), lambda bi, ti, e: (jnp.minimum(e, nblk - 1), 0)),
                  pl.BlockSpec((1, d, eb), lambda bi, ti, e: (jnp.clip(e - 2, 0, nblk - 1), 0, 0)),
                  tab_spec, tab_spec,
                  pl.BlockSpec((1, PEER_HEADS, 1, tt), lambda bi, ti, e: (bi, 0, 0, ti)),
                  pl.BlockSpec((1, tt, oc), lambda bi, ti, e: (bi, ti, chunk(e))),
                  pl.BlockSpec((1, 6, oc), lambda bi, ti, e: (bi, 0, chunk(e)))],
        out_specs=pl.BlockSpec((1, tt, oc), lambda bi, ti, e: (bi, ti, chunk(e))),
        out_shape=jax.ShapeDtypeStruct((b, s, d), F32),
        scratch_shapes=[pltpu.VMEM((d, tt), F32), pltpu.VMEM((eb, tt), F32),
                        pltpu.VMEM((eb, tt), BF16)],
        compiler_params=_params(("arbitrary", "arbitrary", "arbitrary")),
        name="peer_dense",
    )(h2, u, vt, a1, a2, tau, x1, mod)


def _layer(x, c, w_ada, b_ada, norm1_g, w_in, b_f, q_norm_a, k_norm_a, q_norm_b, k_norm_b,
           rel_bias, w_o, norm2_g, w_pq, sk1, sk2, expert_u, expert_v):
    b, s, d = x.shape
    heads = d // HEAD_DIM
    ha = heads // 2
    hb = heads - ha
    assert ha == hb and s % CHUNK == 0
    width = ha * HEAD_DIM

    c_pad = jnp.zeros((8, d), F32).at[:b].set(c)
    mod = _adaln(c_pad, w_ada, b_ada)[:b].reshape(b, 6, d)

    w_qkv = w_in.astype(BF16)
    w_f = jnp.zeros((d, HEAD_DIM), BF16).at[:, :hb].set(w_qkv[:, 6 * width:])
    b_f_pad = jnp.zeros((1, HEAD_DIM), F32).at[0, :hb].set(b_f)
    scale = HEAD_DIM ** -0.5
    ones = jnp.ones((HEAD_DIM,), F32)
    gains = jnp.stack([q_norm_a * scale, k_norm_a, ones,
                       q_norm_b * (scale * LOG2E), k_norm_b, ones])
    qkv, f_cum, *views = _inproj(x, mod, norm1_g.reshape(1, d), w_qkv, w_f, b_f_pad,
                                 gains.reshape(6, 1, HEAD_DIM))

    bias = _bias_tiles(rel_bias, ha)
    out_a = _dilated([qkv] + views, bias)
    out_b = _fox(qkv, f_cum)
    x1 = _outproj(out_a, out_b, w_o.astype(BF16), x, mod)

    h2, a1, a2, tau = _peer_front(x1, mod, norm2_g.reshape(1, d), w_pq.astype(BF16),
                                  sk1.astype(BF16), sk2.astype(BF16))
    vt = jnp.transpose(expert_v.reshape(-1, PEER_EB, d), (0, 2, 1)).astype(BF16)
    return _peer_dense(h2, expert_u.astype(BF16), vt, a1, a2, tau, x1, mod)


def kernel(x, c, w_ada, b_ada, norm1_g, w_in, b_f, q_norm_a, k_norm_a, q_norm_b, k_norm_b, rel_bias,
           w_o, norm2_g, w_pq, sub_keys_1, sub_keys_2, expert_u, expert_v):
    for l in range(w_ada.shape[0]):
        x = _layer(x, c, w_ada[l], b_ada[l], norm1_g[l], w_in[l], b_f[l], q_norm_a[l], k_norm_a[l],
                   q_norm_b[l], k_norm_b[l], rel_bias, w_o[l], norm2_g[l], w_pq[l],
                   sub_keys_1[l], sub_keys_2[l], expert_u[l], expert_v[l])
    return x
```

```python
import functools
import math

import jax
import jax.numpy as jnp
from jax import lax
from jax.experimental import pallas as pl
from jax.experimental.pallas import tpu as pltpu

F32 = jnp.float32
BF16 = jnp.bfloat16

HEAD_DIM = 128
DILATED_BRANCHES = ((128, 1), (512, 4), (2048, 16))
BLOCK_Q = 128
CHUNK = BLOCK_Q * 16
DILATED_GROUP = 16
NUM_BUCKETS = 32
MAX_DISTANCE = 2048
PEER_HEADS = 8
N_KEYS = 128
PEER_TOPK = 16
NORM_EPS = 1e-6
NEG_INF = -1e30
LOG2E = 1.4426950408889634
FOX_BOUND_MAX = 48.0
VMEM_LIMIT = 62 * 1024 * 1024

_NT = (((1,), (1,)), ((), ()))


def _params(sem):
    return pltpu.CompilerParams(dimension_semantics=sem, vmem_limit_bytes=VMEM_LIMIT)


def _adaln_kernel(c_ref, w_ref, b_ref, o_ref):
    c = c_ref[...]
    s = c * jax.nn.sigmoid(c)
    o_ref[...] = jnp.dot(s.astype(BF16), w_ref[...].astype(BF16),
                         preferred_element_type=F32) + b_ref[...]


def _adaln(c_pad, w_ada, b_ada, tn=512):
    rows, d = c_pad.shape
    n = w_ada.shape[1]
    return pl.pallas_call(
        _adaln_kernel,
        grid=(n // tn,),
        in_specs=[pl.BlockSpec((rows, d), lambda j: (0, 0)),
                  pl.BlockSpec((d, tn), lambda j: (0, j)),
                  pl.BlockSpec((1, tn), lambda j: (0, j))],
        out_specs=pl.BlockSpec((rows, tn), lambda j: (0, j)),
        out_shape=jax.ShapeDtypeStruct((rows, n), F32),
        compiler_params=_params(("arbitrary",)),
        name="adaln",
    )(c_pad, w_ada, b_ada.reshape(1, n))


def _t5_bucket(dist):
    max_exact = NUM_BUCKETS // 2
    d32 = jnp.maximum(dist, 1).astype(F32)
    large = max_exact + (jnp.log(d32 / max_exact) / math.log(MAX_DISTANCE / max_exact)
                         * (NUM_BUCKETS - max_exact)).astype(jnp.int32)
    large = jnp.minimum(large, NUM_BUCKETS - 1)
    return jnp.where(dist < max_exact, dist, large)


def _bucket_tiles():
    tiles = []
    for window, dilation in DILATED_BRANCHES:
        nw = window // dilation
        rel = jnp.arange(BLOCK_Q)[:, None] + nw - jnp.arange(BLOCK_Q + nw)[None, :]
        in_win = (rel >= 0) & (rel <= nw)
        bucket = _t5_bucket(jnp.clip(rel, 0, nw) * dilation)
        tiles.append(jnp.where(in_win, bucket, -1).astype(jnp.int32))
    return jnp.stack(tiles)


def _bias_kernel(rb_ref, bucket_ref, o_ref):
    h = pl.program_id(1)
    bucket = bucket_ref[0]
    acc = jnp.full(bucket.shape, NEG_INF, F32)
    for b in range(NUM_BUCKETS):
        acc = jnp.where(bucket == b, rb_ref[b, h], acc)
    o_ref[0, 0] = acc


def _bias_tiles(rel_bias, heads_a):
    buckets = _bucket_tiles()
    nb, bq, bk = buckets.shape
    return pl.pallas_call(
        _bias_kernel,
        grid=(nb, heads_a),
        in_specs=[pl.BlockSpec(memory_space=pltpu.SMEM),
                  pl.BlockSpec((1, bq, bk), lambda d, h: (d, 0, 0))],
        out_specs=pl.BlockSpec((1, 1, bq, bk), lambda d, h: (d, h, 0, 0)),
        out_shape=jax.ShapeDtypeStruct((nb, heads_a, bq, bk), F32),
        compiler_params=_params(("arbitrary", "arbitrary")),
        name="bias_tiles",
    )(rel_bias, buckets)


def _split3(x):
    hi = x.astype(BF16)
    r = x - hi.astype(F32)
    mid = r.astype(BF16)
    lo = (r - mid.astype(F32)).astype(BF16)
    return hi, mid, lo


def _inproj_kernel(x_ref, mod_ref, g_ref, w_ref, wf_ref, bf_ref, gain_ref,
                   qkv_ref, f_ref, *rest, tm, tn, seg_tiles):
    dils = [d for _, d in DILATED_BRANCHES if d > 1]
    view_refs = rest[:len(dils)]
    h_scr, carry_scr, y_scr = rest[len(dils):]
    si = pl.program_id(1)
    j = pl.program_id(2)

    @pl.when(j == 0)
    def _():
        @pl.when(si == 0)
        def _():
            carry_scr[...] = jnp.zeros_like(carry_scr)

        rc = min(tm, 256)
        row = lax.broadcasted_iota(jnp.int32, (rc, rc), 0)
        col = lax.broadcasted_iota(jnp.int32, (rc, rc), 1)
        tri = jnp.where(col <= row, 1.0, 0.0).astype(BF16)

        def chunk(ci, carry):
            r0 = pl.multiple_of(ci * rc, rc)
            x = x_ref[0, pl.ds(r0, rc), :]
            ms = jnp.mean(x * x, axis=-1, keepdims=True)
            y = x * lax.rsqrt(ms + NORM_EPS) * g_ref[...]
            h = y * (1.0 + mod_ref[0, 1:2, :]) + mod_ref[0, 0:1, :]
            hb = h.astype(BF16)
            h_scr[pl.ds(r0, rc), :] = hb
            fz = jnp.dot(hb, wf_ref[...], preferred_element_type=F32) + bf_ref[...]
            lf = jnp.minimum(fz, 0.0) - jnp.log(1.0 + jnp.exp(-jnp.abs(fz)))
            hi, mid, lo = _split3(lf)
            cs = (jnp.dot(tri, hi, preferred_element_type=F32)
                  + jnp.dot(tri, mid, preferred_element_type=F32)
                  + jnp.dot(tri, lo, preferred_element_type=F32)) + carry_scr[0:1, :]
            f_ref[0, pl.ds(r0, rc), :] = cs
            carry_scr[0:1, :] = cs[rc - 1:rc, :]
            return carry

        lax.fori_loop(0, tm // rc, chunk, 0)

    acc = jnp.dot(h_scr[...], w_ref[...], preferred_element_type=F32)
    seg = j // seg_tiles
    is_norm = jnp.logical_and(seg != 2, seg != 5)
    gain = gain_ref[0]
    for hh in range(tn // HEAD_DIM):
        y = acc[:, hh * HEAD_DIM:(hh + 1) * HEAD_DIM]
        ms = jnp.mean(y * y, axis=-1, keepdims=True)
        scale = jnp.where(is_norm, lax.rsqrt(ms + NORM_EPS), 1.0)
        yn = y * scale * gain
        qkv_ref[0, 0, hh] = yn.astype(BF16)
        y_scr[hh * tm:(hh + 1) * tm, :] = yn

    @pl.when(seg < 3)
    def _():
        for vref, d in zip(view_refs, dils):
            for hh in range(tn // HEAD_DIM):
                for r in range(d):
                    vref[0, 0, hh, :, r * HEAD_DIM:(r + 1) * HEAD_DIM] = (
                        y_scr[pl.ds(hh * tm + r, tm // d, stride=d), :].astype(BF16))


def _inproj(x, mod, norm1_g, w_qkv, w_f, b_f, gains, tm=1024, tn=512):
    b, s, d = x.shape
    n = 3 * d
    width = n // 6
    hg = width // HEAD_DIM
    tn = min(tn, width)
    tm = min(tm, s)
    seg_tiles = width // tn
    hpt = tn // HEAD_DIM
    kern = functools.partial(_inproj_kernel, tm=tm, tn=tn, seg_tiles=seg_tiles)
    last_a = 3 * seg_tiles - 1

    def view_map(bi, si, j):
        ja = jnp.minimum(j, last_a)
        return (ja // seg_tiles, bi, ja % seg_tiles, si, 0)

    dils = [d for _, d in DILATED_BRANCHES if d > 1]
    view_specs = [pl.BlockSpec((1, 1, hpt, tm // d, d * HEAD_DIM), view_map) for d in dils]
    view_shapes = [jax.ShapeDtypeStruct((3, b, hg, s // d, d * HEAD_DIM), BF16) for d in dils]
    return pl.pallas_call(
        kern,
        grid=(b, s // tm, n // tn),
        in_specs=[pl.BlockSpec((1, tm, d), lambda bi, si, j: (bi, si, 0),
                               pipeline_mode=pl.Buffered(1)),
                  pl.BlockSpec((1, 6, d), lambda bi, si, j: (bi, 0, 0)),
                  pl.BlockSpec((1, d), lambda bi, si, j: (0, 0)),
                  pl.BlockSpec((d, tn), lambda bi, si, j: (0, j)),
                  pl.BlockSpec((d, HEAD_DIM), lambda bi, si, j: (0, 0)),
                  pl.BlockSpec((1, HEAD_DIM), lambda bi, si, j: (0, 0)),
                  pl.BlockSpec((1, 1, HEAD_DIM), lambda bi, si, j: (j // seg_tiles, 0, 0))],
        out_specs=[pl.BlockSpec((1, 1, hpt, tm, HEAD_DIM),
                                lambda bi, si, j: (j // seg_tiles, bi, j % seg_tiles, si, 0)),
                   pl.BlockSpec((1, tm, HEAD_DIM), lambda bi, si, j: (bi, si, 0))] + view_specs,
        out_shape=[jax.ShapeDtypeStruct((6, b, hg, s, HEAD_DIM), BF16),
                   jax.ShapeDtypeStruct((b, s, HEAD_DIM), F32)] + view_shapes,
        scratch_shapes=[pltpu.VMEM((tm, d), BF16), pltpu.VMEM((8, HEAD_DIM), F32),
                        pltpu.VMEM((hpt * tm, HEAD_DIM), F32)],
        compiler_params=_params(("arbitrary", "arbitrary", "arbitrary")),
        name="inproj",
    )(x, mod, norm1_g, w_qkv, w_f, b_f, gains)


def _dilated_kernel(*refs):
    nb = len(DILATED_BRANCHES)
    q_refs, kc_refs, kp_refs, vc_refs, vp_refs = (refs[i * nb:(i + 1) * nb] for i in range(5))
    bias_ref = refs[5 * nb]
    o_ref = refs[5 * nb + 1]
    scr = refs[5 * nb + 2:]
    kf_scr, vf_scr, num_scr, m_scr, den_scr = (scr[i * nb:(i + 1) * nb] for i in range(5))
    first_chunk = pl.program_id(2) == 0
    bq = BLOCK_Q
    hd = HEAD_DIM
    bnt = (((2,), (2,)), ((0,), (0,)))
    bnn = (((2,), (1,)), ((0,), (0,)))

    for bi, (_, d) in enumerate(DILATED_BRANCHES):
        rows = CHUNK // d
        nblk = rows // bq
        kf_scr[bi][0:bq, :] = kp_refs[bi][0, 0, 0]
        kf_scr[bi][bq:, :] = kc_refs[bi][0, 0, 0]
        vf_scr[bi][0:bq, :] = vp_refs[bi][0, 0, 0]
        vf_scr[bi][bq:, :] = vc_refs[bi][0, 0, 0]
        all_blocks = [(r, n) for r in range(d) for n in range(nblk)]
        ng = DILATED_GROUP
        for g0 in range(0, len(all_blocks), ng):
            blocks = all_blocks[g0:g0 + ng]
            q = jnp.stack([q_refs[bi][0, 0, 0, n * bq:(n + 1) * bq, r * hd:(r + 1) * hd]
                           for r, n in blocks])
            kw = jnp.stack([kf_scr[bi][n * bq:(n + 2) * bq, r * hd:(r + 1) * hd] for r, n in blocks])
            vw = jnp.stack([vf_scr[bi][n * bq:(n + 2) * bq, r * hd:(r + 1) * hd] for r, n in blocks])
            s = lax.dot_general(q, kw, bnt, preferred_element_type=F32)
            gi = g0 + lax.broadcasted_iota(jnp.int32, (ng, 1, 2 * bq), 0)
            col = lax.broadcasted_iota(jnp.int32, (ng, 1, 2 * bq), 2)
            lim = jnp.where(gi % nblk == 0, jnp.where(first_chunk, bq, 0), 0)
            kill = jnp.where(col < lim, NEG_INF, 0.0)
            logits = s + bias_ref[bi, 0] + kill
            m = jnp.max(logits, axis=-1, keepdims=True)
            p = jnp.exp(logits - m)
            den = jnp.sum(p, axis=-1, keepdims=True)
            num = lax.dot_general(p.astype(BF16), vw, bnn, preferred_element_type=F32)
            for g, (r, n) in enumerate(blocks):
                sel = pl.ds(n * bq * d + r, bq, stride=d) if d > 1 else pl.ds(n * bq, bq)
                num_scr[bi][sel, :] = num[g]
                m_scr[bi][sel, :] = jnp.broadcast_to(m[g], (bq, hd))
                den_scr[bi][sel, :] = jnp.broadcast_to(den[g], (bq, hd))

    rc = 512
    for c in range(CHUNK // rc):
        rows = slice(c * rc, (c + 1) * rc)
        ms = [m_scr[bi][rows, :] for bi in range(nb)]
        m_all = functools.reduce(jnp.maximum, ms)
        num_t = None
        den_t = None
        for bi in range(nb):
            w = jnp.exp(ms[bi] - m_all)
            nw = num_scr[bi][rows, :] * w
            dw = den_scr[bi][rows, :] * w
            num_t = nw if num_t is None else num_t + nw
            den_t = dw if den_t is None else den_t + dw
        o_ref[0, 0, rows, :] = (num_t / den_t).astype(o_ref.dtype)


def _dilated(views, bias):
    _, b, ha, s, _ = views[0].shape
    nchunk = s // CHUNK
    q_specs, kc_specs, kp_specs, vc_specs, vp_specs = [], [], [], [], []
    scratch_kv = []
    for (window, d), view in zip(DILATED_BRANCHES, views):
        assert window // d == BLOCK_Q and view.shape[3:] == (s // d, d * HEAD_DIM)
        rows = CHUNK // d
        blk = (1, 1, 1, rows, d * HEAD_DIM)
        pblk = (1, 1, 1, BLOCK_Q, d * HEAD_DIM)
        prev = rows // BLOCK_Q

        def cur_map(seg):
            return lambda bi, h, c: (seg, bi, h, c, 0)

        def prev_map(seg, prev=prev):
            return lambda bi, h, c: (seg, bi, h, jnp.maximum(c * prev - 1, 0), 0)

        q_specs.append(pl.BlockSpec(blk, cur_map(0)))
        kc_specs.append(pl.BlockSpec(blk, cur_map(1)))
        kp_specs.append(pl.BlockSpec(pblk, prev_map(1)))
        vc_specs.append(pl.BlockSpec(blk, cur_map(2)))
        vp_specs.append(pl.BlockSpec(pblk, prev_map(2)))
        scratch_kv.append(pltpu.VMEM((rows + BLOCK_Q, d * HEAD_DIM), BF16))
    nb = len(DILATED_BRANCHES)
    scratch = scratch_kv + scratch_kv + [pltpu.VMEM((CHUNK, HEAD_DIM), F32)] * (3 * nb)
    bias_spec = pl.BlockSpec((nb, 1, BLOCK_Q, 2 * BLOCK_Q), lambda bi, h, c: (0, h, 0, 0))
    return pl.pallas_call(
        _dilated_kernel,
        grid=(b, ha, nchunk),
        in_specs=q_specs + kc_specs + kp_specs + vc_specs + vp_specs + [bias_spec],
        out_specs=pl.BlockSpec((1, 1, CHUNK, HEAD_DIM), lambda bi, h, c: (bi, h, c, 0)),
        out_shape=jax.ShapeDtypeStruct((b, ha, s, HEAD_DIM), BF16),
        scratch_shapes=scratch,
        compiler_params=_params(("arbitrary", "arbitrary", "arbitrary")),
        name="dilated",
    )(*(list(views) * 5), bias)


def _lane_cols(cols, n):
    lane = lax.broadcasted_iota(jnp.int32, (n, HEAD_DIM), 1)
    out = jnp.zeros((n, HEAD_DIM), F32)
    for c, col in enumerate(cols):
        out = jnp.where(lane == c, col, out)
    return out


def _fox_kernel(q_ref, k_ref, v_ref, f_ref, o_ref, ka_scr, va_scr, qa_scr, m_scr, acc_scr, kn_scr,
                *, tq):
    h = pl.program_id(1)
    qi = pl.program_id(2)
    s_len = k_ref.shape[3]
    hd = HEAD_DIM

    def f_col(r0, n):
        ftile = f_ref[0, pl.ds(r0, n), :]
        lane = lax.broadcasted_iota(jnp.int32, ftile.shape, 1)
        return jnp.sum(jnp.where(lane == h, ftile, 0.0), axis=-1, keepdims=True) * LOG2E

    def parts(x):
        return [p.astype(F32) for p in _split3(x)]

    @pl.when(qi == 0)
    def _():
        kn_scr[...] = jnp.zeros(kn_scr.shape, F32)

        def build(ci, carry):
            r0 = pl.multiple_of(ci * tq, tq)
            ext = _lane_cols(parts(-f_col(r0, tq)) + [1.0] * 6, tq)
            k = k_ref[0, 0, 0, pl.ds(r0, tq), :]
            k32 = k.astype(F32)
            kn = jnp.max(jnp.sum(k32 * k32, axis=-1, keepdims=True), axis=0, keepdims=True)
            kn_scr[...] = jnp.maximum(kn_scr[...], jnp.broadcast_to(kn, kn_scr.shape))
            ka_scr[pl.ds(r0, tq), 0:hd] = k
            ka_scr[pl.ds(r0, tq), hd:2 * hd] = ext.astype(BF16)
            va_scr[pl.ds(r0, tq), 0:hd] = v_ref[0, 0, 0, pl.ds(r0, tq), :]
            va_scr[pl.ds(r0, tq), hd:2 * hd] = _lane_cols([1.0], tq).astype(BF16)
            return carry

        lax.fori_loop(0, s_len // tq, build, 0)

    q0 = pl.multiple_of(qi * tq, tq)
    qcols = [1.0] * 3 + parts(f_col(q0, tq))
    qa_scr[:, 0:hd] = q_ref[0, 0, 0]
    qa_scr[:, hd:2 * hd] = _lane_cols(qcols, tq).astype(BF16)

    def scores(k0, width):
        return lax.dot_general(qa_scr[...], ka_scr[pl.ds(k0, width), :], _NT,
                               preferred_element_type=F32)

    def causal_scores(k0, width):
        row = lax.broadcasted_iota(jnp.int32, (tq, width), 0)
        col = lax.broadcasted_iota(jnp.int32, (tq, width), 1)
        return jnp.where(col <= row + (width - tq), scores(k0, width), NEG_INF)

    def sweep(tile):
        def pair(j, carry):
            k0 = pl.multiple_of(j * 2 * tq, 2 * tq)
            tile(scores(k0, 2 * tq), k0, 2 * tq)
            return carry

        lax.fori_loop(0, qi // 2, pair, 0)

        @pl.when(qi % 2 == 1)
        def _():
            k0 = pl.multiple_of((qi - 1) * tq, tq)
            tile(causal_scores(k0, 2 * tq), k0, 2 * tq)

        @pl.when(qi % 2 == 0)
        def _():
            tile(causal_scores(q0, tq), q0, tq)

    def fold_max(s):
        mr = m_scr[...]
        for c in range(s.shape[1] // hd):
            mr = jnp.maximum(mr, s[:, c * hd:(c + 1) * hd])
        m_scr[...] = mr

    q32 = q_ref[0, 0, 0].astype(F32)
    bound = jnp.sqrt(jnp.sum(q32 * q32, axis=-1, keepdims=True) * kn_scr[0:1, 0:1])
    in_range = jnp.max(bound) < FOX_BOUND_MAX

    @pl.when(in_range)
    def _():
        qa_scr[:, hd:2 * hd] = _lane_cols(qcols + parts(-bound), tq).astype(BF16)

    @pl.when(jnp.logical_not(in_range))
    def _():
        m_scr[...] = jnp.full(m_scr.shape, NEG_INF, F32)
        sweep(lambda s, k0, width: fold_max(s))
        m = jnp.max(m_scr[...], axis=-1, keepdims=True)
        qa_scr[:, hd:2 * hd] = _lane_cols(qcols + parts(-m), tq).astype(BF16)

    acc_scr[...] = jnp.zeros(acc_scr.shape, F32)

    def pv_tile(s, k0, width):
        p = jnp.exp2(s).astype(BF16)
        acc_scr[...] += jnp.dot(p, va_scr[pl.ds(k0, width), :], preferred_element_type=F32)

    sweep(pv_tile)
    acc = acc_scr[...]
    o_ref[0, 0] = (acc[:, 0:hd] / acc[:, hd:hd + 1]).astype(o_ref.dtype)


def _fox(qkv, f_cum, tq=512):
    _, b, hb, s, _ = qkv.shape
    tq = min(tq, s)
    kern = functools.partial(_fox_kernel, tq=tq)
    return pl.pallas_call(
        kern,
        grid=(b, hb, s // tq),
        in_specs=[pl.BlockSpec((1, 1, 1, tq, HEAD_DIM), lambda bi, h, qi: (3, bi, h, qi, 0)),
                  pl.BlockSpec((1, 1, 1, s, HEAD_DIM), lambda bi, h, qi: (4, bi, h, 0, 0)),
                  pl.BlockSpec((1, 1, 1, s, HEAD_DIM), lambda bi, h, qi: (5, bi, h, 0, 0)),
                  pl.BlockSpec((1, s, HEAD_DIM), lambda bi, h, qi: (bi, 0, 0))],
        out_specs=pl.BlockSpec((1, 1, tq, HEAD_DIM), lambda bi, h, qi: (bi, h, qi, 0)),
        out_shape=jax.ShapeDtypeStruct((b, hb, s, HEAD_DIM), BF16),
        scratch_shapes=[pltpu.VMEM((s, 2 * HEAD_DIM), BF16), pltpu.VMEM((s, 2 * HEAD_DIM), BF16),
                        pltpu.VMEM((tq, 2 * HEAD_DIM), BF16), pltpu.VMEM((tq, HEAD_DIM), F32),
                        pltpu.VMEM((tq, 2 * HEAD_DIM), F32), pltpu.VMEM((8, HEAD_DIM), F32)],
        compiler_params=_params(("arbitrary", "arbitrary", "arbitrary")),
        name="fox",
    )(qkv, qkv, qkv, f_cum)


def _outproj_kernel(oa_ref, ob_ref, w_ref, x_ref, mod_ref, o_ref, mix_scr):
    j = pl.program_id(2)
    ha = oa_ref.shape[1]
    hb = ob_ref.shape[1]

    @pl.when(j == 0)
    def _():
        for h in range(ha):
            mix_scr[:, h * HEAD_DIM:(h + 1) * HEAD_DIM] = oa_ref[0, h]
        for h in range(hb):
            mix_scr[:, (ha + h) * HEAD_DIM:(ha + h + 1) * HEAD_DIM] = ob_ref[0, h]

    acc = jnp.dot(mix_scr[...], w_ref[...], preferred_element_type=F32)
    o_ref[0] = x_ref[0] + mod_ref[0, 2:3, :] * acc


def _outproj(out_a, out_b, w_o, x, mod, tm=1024, tn=512):
    b, s, d = x.shape
    ha, hb = out_a.shape[1], out_b.shape[1]
    tm = min(tm, s)
    tn = min(tn, d)
    return pl.pallas_call(
        _outproj_kernel,
        grid=(b, s // tm, d // tn),
        in_specs=[pl.BlockSpec((1, ha, tm, HEAD_DIM), lambda bi, si, j: (bi, 0, si, 0)),
                  pl.BlockSpec((1, hb, tm, HEAD_DIM), lambda bi, si, j: (bi, 0, si, 0)),
                  pl.BlockSpec((d, tn), lambda bi, si, j: (0, j)),
                  pl.BlockSpec((1, tm, tn), lambda bi, si, j: (bi, si, j)),
                  pl.BlockSpec((1, 6, tn), lambda bi, si, j: (bi, 0, j))],
        out_specs=pl.BlockSpec((1, tm, tn), lambda bi, si, j: (bi, si, j)),
        out_shape=jax.ShapeDtypeStruct((b, s, d), F32),
        scratch_shapes=[pltpu.VMEM((tm, d), BF16)],
        compiler_params=_params(("arbitrary", "arbitrary", "arbitrary")),
        name="outproj",
    )(out_a, out_b, w_o, x, mod)


def _sort_network(n):
    pairs = []
    p = 1
    while p < n:
        k = p
        while k >= 1:
            for j in range(k % p, n - k, 2 * k):
                for i in range(min(k, n - j - k)):
                    if (i + j) // (2 * p) == (i + j + k) // (2 * p):
                        pairs.append((i + j, i + j + k))
            k //= 2
        p *= 2
    return pairs


def _top_desc_sorted_lists(s, k, sub=8):
    lists = [s[i * sub:(i + 1) * sub, :] for i in range(k)]
    for i, j in _sort_network(k):
        hi = jnp.maximum(lists[i], lists[j])
        lists[j] = jnp.minimum(lists[i], lists[j])
        lists[i] = hi
    vals = []
    for t in range(k):
        head = lists[0]
        mx = jnp.max(head, axis=0, keepdims=True)
        vals.append(mx)
        taken = head >= mx
        for i in range(k - 1 - t):
            lists[i] = jnp.where(taken, lists[i + 1], lists[i])
    return vals


def _peer_front_kernel(x_ref, mod_ref, g_ref, w_ref, sk1_ref, sk2_ref,
                       h2_ref, a1_ref, a2_ref, tau_ref, h_scr):
    hh = pl.program_id(2)

    @pl.when(hh == 0)
    def _():
        x = x_ref[0]
        ms = jnp.mean(x * x, axis=-1, keepdims=True)
        y = x * lax.rsqrt(ms + NORM_EPS) * g_ref[...]
        h = (y * (1.0 + mod_ref[0, 4:5, :]) + mod_ref[0, 3:4, :]).astype(BF16)
        h_scr[...] = h
        h2_ref[0] = h

    q = jnp.dot(h_scr[...], w_ref[...], preferred_element_type=F32)
    q1 = q[:, :N_KEYS].astype(BF16)
    q2 = q[:, N_KEYS:].astype(BF16)
    s1 = lax.dot_general(sk1_ref[...], q1, _NT, preferred_element_type=F32)
    s2 = lax.dot_general(sk2_ref[...], q2, _NT, preferred_element_type=F32)
    v1 = _top_desc_sorted_lists(s1, PEER_TOPK)
    v2 = _top_desc_sorted_lists(s2, PEER_TOPK)
    v1_all = jnp.concatenate(v1, axis=0)
    v2_all = jnp.concatenate(v2, axis=0)
    half = PEER_TOPK // 2
    cands = [v1[0] + v2_all]
    cands += [v1[a] + v2_all[0:half, :] for a in range(1, half)]
    cands += [v1_all[half:, :] + v2[0]]
    real = jnp.concatenate(cands, axis=0)
    pad_rows = PEER_TOPK * 8 - real.shape[0]
    cand = jnp.concatenate([real, jnp.full((pad_rows, s1.shape[1]), -jnp.inf, F32)], axis=0)
    top = _top_desc_sorted_lists(cand, PEER_TOPK)
    tau = top[-1]
    z = None
    for t in top:
        e = jnp.exp(t - top[0])
        z = e if z is None else z + e

    def f1(s):
        return 0.5 * jnp.exp(s - v1[0]) / z

    def f2(s):
        return jnp.exp(s - v2[0])

    g1 = f1(v1_all)
    g2 = f2(v2_all)
    prods = [g1[0:1, :] * g2]
    prods += [g1[a:a + 1, :] * g2[0:half, :] for a in range(1, half)]
    prods += [g1[half:, :] * g2[0:1, :]]
    prod = jnp.concatenate(prods, axis=0)
    a1_ref[0, 0] = f1(s1)
    a2_ref[0, 0] = f2(s2)
    tau_ref[0, 0] = jnp.min(jnp.where(real >= tau, prod, jnp.inf), axis=0, keepdims=True)


def _peer_front(x1, mod, norm2_g, w_pq, sk1, sk2, tm=512):
    b, s, d = x1.shape
    tm = min(tm, s)
    qd = 2 * N_KEYS
    tab = jax.ShapeDtypeStruct((b, PEER_HEADS, N_KEYS, s), F32)
    tab_spec = pl.BlockSpec((1, 1, N_KEYS, tm), lambda bi, si, hh: (bi, hh, 0, si))
    return pl.pallas_call(
        _peer_front_kernel,
        grid=(b, s // tm, PEER_HEADS),
        in_specs=[pl.BlockSpec((1, tm, d), lambda bi, si, hh: (bi, si, 0)),
                  pl.BlockSpec((1, 6, d), lambda bi, si, hh: (bi, 0, 0)),
                  pl.BlockSpec((1, d), lambda bi, si, hh: (0, 0)),
                  pl.BlockSpec((d, qd), lambda bi, si, hh: (0, hh)),
                  pl.BlockSpec((N_KEYS, N_KEYS), lambda bi, si, hh: (0, 0)),
                  pl.BlockSpec((N_KEYS, N_KEYS), lambda bi, si, hh: (0, 0))],
        out_specs=[pl.BlockSpec((1, tm, d), lambda bi, si, hh: (bi, si, 0)),
                   tab_spec, tab_spec,
                   pl.BlockSpec((1, 1, 1, tm), lambda bi, si, hh: (bi, hh, 0, si))],
        out_shape=[jax.ShapeDtypeStruct((b, s, d), BF16), tab, tab,
                   jax.ShapeDtypeStruct((b, PEER_HEADS, 1, s), F32)],
        scratch_shapes=[pltpu.VMEM((tm, d), BF16)],
        compiler_params=_params(("arbitrary", "arbitrary", "arbitrary")),
        name="peer_front",
    )(x1, mod, norm2_g, w_pq, sk1, sk2)


def _gelu_x2(a):
    return a * (1.0 + lax.erf(a * (2.0 ** -0.5)))


def _peer_dense_kernel(h2_ref, u_ref, vt_ref, a1_ref, a2_ref, tau_ref,
                       x_ref, mod_ref, o_ref, acc_scr, a_scr, act_scr, *, tt, eb, nblk):
    e = pl.program_id(2)
    rows_per_blk = eb // N_KEYS

    @pl.when(e == 0)
    def _():
        acc_scr[...] = jnp.zeros(acc_scr.shape, F32)
        a_scr[...] = jnp.zeros(a_scr.shape, F32)

    def stage3():
        acc_scr[...] += jnp.dot(vt_ref[0], act_scr[...], preferred_element_type=F32)

    def stage1():
        a_scr[...] = lax.dot_general(u_ref[...], h2_ref[0], _NT, preferred_element_type=F32)

    def stage2():
        blk = jnp.clip(e - 1, 0, nblk - 1)
        a1_rows = [[a1_ref[0, hh, pl.ds(blk * rows_per_blk + il, 1), :] for hh in range(PEER_HEADS)]
                   for il in range(rows_per_blk)]
        for tg in range(tt // 128):
            lanes = slice(tg * 128, (tg + 1) * 128)
            for il in range(rows_per_blk):
                rows = slice(il * N_KEYS, (il + 1) * N_KEYS)
                gate = jnp.zeros((N_KEYS, 128), F32)
                for hh in range(PEER_HEADS):
                    w = a1_rows[il][hh][:, lanes] * a2_ref[0, hh, :, lanes]
                    gate = gate + jnp.where(w >= tau_ref[0, hh, :, lanes], w, 0.0)
                act_scr[rows, lanes] = (_gelu_x2(a_scr[rows, lanes]) * gate).astype(BF16)

    @pl.when(e == 0)
    def _():
        act_scr[...] = jnp.zeros(act_scr.shape, BF16)

    @pl.when(e < 2)
    def _():
        stage2()
        stage1()

    @pl.when(jnp.logical_and(e >= 2, e <= nblk + 1))
    def _():
        stage3()
        stage2()
        stage1()

    @pl.when(e > nblk + 1)
    def _():
        oc = o_ref.shape[2]
        r0 = pl.multiple_of((e - (nblk + 2)) * oc, oc)
        o_ref[0] = x_ref[0] + mod_ref[0, 5:6, :] * acc_scr[pl.ds(r0, oc), :].T


PEER_EB = 512


def _peer_dense(h2, u, vt, a1, a2, tau, x1, mod, tt=512, oc=1024):
    b, s, d = x1.shape
    nblk, _, eb = vt.shape
    tt = min(tt, s)
    oc = min(oc, d)
    kern = functools.partial(_peer_dense_kernel, tt=tt, eb=eb, nblk=nblk)
    tab_spec = pl.BlockSpec((1, PEER_HEADS, N_KEYS, tt), lambda bi, ti, e: (bi, 0, 0, ti))

    def chunk(e):
        return jnp.maximum(e - (nblk + 2), 0)

    return pl.pallas_call(
        kern,
        grid=(b, s // tt, nblk + 2 + d // oc),
        in_specs=[pl.BlockSpec((1, tt, d), lambda bi, ti, e: (bi, ti, 0)),
                  pl.BlockSpec((eb, d), lambda bi, ti, e: (jnp.minimum(e, nblk - 1), 0)),
                  pl.BlockSpec((1, d, eb), lambda bi, ti, e: (jnp.clip(e - 2, 0, nblk - 1), 0, 0)),
                  tab_spec, tab_spec,
                  pl.BlockSpec((1, PEER_HEADS, 1, tt), lambda bi, ti, e: (bi, 0, 0, ti)),
                  pl.BlockSpec((1, tt, oc), lambda bi, ti, e: (bi, ti, chunk(e))),
                  pl.BlockSpec((1, 6, oc), lambda bi, ti, e: (bi, 0, chunk(e)))],
        out_specs=pl.BlockSpec((1, tt, oc), lambda bi, ti, e: (bi, ti, chunk(e))),
        out_shape=jax.ShapeDtypeStruct((b, s, d), F32),
        scratch_shapes=[pltpu.VMEM((d, tt), F32), pltpu.VMEM((eb, tt), F32),
                        pltpu.VMEM((eb, tt), BF16)],
        compiler_params=_params(("arbitrary", "arbitrary", "arbitrary")),
        name="peer_dense",
    )(h2, u, vt, a1, a2, tau, x1, mod)


def _layer(x, c, w_ada, b_ada, norm1_g, w_in, b_f, q_norm_a, k_norm_a, q_norm_b, k_norm_b,
           rel_bias, w_o, norm2_g, w_pq, sk1, sk2, expert_u, expert_v):
    b, s, d = x.shape
    heads = d // HEAD_DIM
    ha = heads // 2
    hb = heads - ha
    assert ha == hb and s % CHUNK == 0
    width = ha * HEAD_DIM

    c_pad = jnp.zeros((8, d), F32).at[:b].set(c)
    mod = _adaln(c_pad, w_ada, b_ada)[:b].reshape(b, 6, d)

    w_qkv = w_in.astype(BF16)
    w_f = jnp.zeros((d, HEAD_DIM), BF16).at[:, :hb].set(w_qkv[:, 6 * width:])
    b_f_pad = jnp.zeros((1, HEAD_DIM), F32).at[0, :hb].set(b_f)
    scale = HEAD_DIM ** -0.5
    ones = jnp.ones((HEAD_DIM,), F32)
    gains = jnp.stack([q_norm_a * scale, k_norm_a, ones,
                       q_norm_b * (scale * LOG2E), k_norm_b, ones])
    qkv, f_cum, *views = _inproj(x, mod, norm1_g.reshape(1, d), w_qkv, w_f, b_f_pad,
                                 gains.reshape(6, 1, HEAD_DIM))

    bias = _bias_tiles(rel_bias, ha)
    out_a = _dilated([qkv] + views, bias)
    out_b = _fox(qkv, f_cum)
    x1 = _outproj(out_a, out_b, w_o.astype(BF16), x, mod)

    h2, a1, a2, tau = _peer_front(x1, mod, norm2_g.reshape(1, d), w_pq.astype(BF16),
                                  sk1.astype(BF16), sk2.astype(BF16))
    vt = jnp.transpose(expert_v.reshape(-1, PEER_EB, d), (0, 2, 1)).astype(BF16)
    return _peer_dense(h2, expert_u.astype(BF16), vt, a1, a2, tau, x1, mod)


def kernel(x, c, w_ada, b_ada, norm1_g, w_in, b_f, q_norm_a, k_norm_a, q_norm_b, k_norm_b, rel_bias,
           w_o, norm2_g, w_pq, sub_keys_1, sub_keys_2, expert_u, expert_v):
    for l in range(w_ada.shape[0]):
        x = _layer(x, c, w_ada[l], b_ada[l], norm1_g[l], w_in[l], b_f[l], q_norm_a[l], k_norm_a[l],
                   q_norm_b[l], k_norm_b[l], rel_bias, w_o[l], norm2_g[l], w_pq[l],
                   sub_keys_1[l], sub_keys_2[l], expert_u[l], expert_v[l])
    return x
```
